```python
import jax, jax.numpy as jnp
from jax import lax
import numpy as np

D_MODEL = 2048
BATCH = 1
SEQ = 16384
DEPTH = 2

CHUNK = 64
Q_BLOCK = 128
N_A_LAYERS = DEPTH // 2
N_B_LAYERS = DEPTH - N_A_LAYERS
EPS = 1e-6

GLA_HEADS = 4
GLA_QK_DIM = D_MODEL // 2
GLA_V_DIM = D_MODEL
GLA_DK = GLA_QK_DIM // GLA_HEADS
GLA_DV = GLA_V_DIM // GLA_HEADS
GLA_GATE_RANK = 16
GLA_GATE_TAU = 16.0
GLA_IN_DIM = 2 * GLA_QK_DIM + 2 * GLA_V_DIM + GLA_GATE_RANK

SB_HEADS = 16
SB_HEAD_DIM = D_MODEL // SB_HEADS
SB_KV_HEADS = 4
SB_GROUP = SB_HEADS // SB_KV_HEADS
SB_KV_DIM = SB_KV_HEADS * SB_HEAD_DIM

D_FF = 4 * D_MODEL

kernel_name = "yoco_gla_stickbreaking_hybrid"


def rms_norm(x, g):
    xf = x.astype(jnp.float32)
    y = xf * lax.rsqrt(jnp.mean(xf * xf, axis=-1, keepdims=True) + EPS)
    return (y * g.astype(jnp.float32)).astype(x.dtype)


def sq_relu_mlp(x, w_up, w_down):
    return jnp.square(jax.nn.relu(x @ w_up)) @ w_down


def _to_chunks(t, n_heads, d):
    b, s, _ = t.shape
    return t.reshape(b, s // CHUNK, CHUNK, n_heads, d).transpose(1, 0, 3, 2, 4)


def gla_mixer(x, w_in, w_gate_up, b_gate, g_head, w_out):
    b, s, _ = x.shape
    proj = x @ w_in
    i0 = GLA_QK_DIM
    i1 = 2 * GLA_QK_DIM
    i2 = i1 + GLA_V_DIM
    i3 = i2 + GLA_V_DIM
    q = proj[..., :i0].astype(jnp.float32) * (GLA_DK ** -0.5)
    k = proj[..., i0:i1].astype(jnp.float32)
    v = proj[..., i1:i2].astype(jnp.float32)
    r = proj[..., i2:i3]
    a_low = proj[..., i3:]
    log_alpha = jax.nn.log_sigmoid((a_low @ w_gate_up + b_gate).astype(jnp.float32)) / GLA_GATE_TAU

    qc = _to_chunks(q, GLA_HEADS, GLA_DK)
    kc = _to_chunks(k, GLA_HEADS, GLA_DK)
    vc = _to_chunks(v, GLA_HEADS, GLA_DV)
    gc = _to_chunks(log_alpha, GLA_HEADS, GLA_DK)
    causal = jnp.tril(jnp.ones((CHUNK, CHUNK), dtype=bool))

    def step(state, inp):
        q_i, k_i, v_i, g_i = inp
        cum = jnp.cumsum(g_i, axis=-2)
        o_inter = jnp.einsum('bhik,bhkv->bhiv', q_i * jnp.exp(cum), state)
        diff = cum[:, :, :, None, :] - cum[:, :, None, :, :]
        decay = jnp.exp(jnp.where(causal[:, :, None], diff, -jnp.inf))
        scores = jnp.einsum('bhik,bhjk,bhijk->bhij', q_i, k_i, decay)
        o_intra = jnp.einsum('bhij,bhjv->bhiv', scores, v_i)
        last = cum[:, :, -1:, :]
        new_state = (jnp.exp(last[:, :, 0, :])[..., None] * state
                     + jnp.einsum('bhjk,bhjv->bhkv', k_i * jnp.exp(last - cum), v_i))
        return new_state, o_inter + o_intra

    s0 = jnp.zeros((b, GLA_HEADS, GLA_DK, GLA_DV), jnp.float32)
    _, o = lax.scan(step, s0, (qc, kc, vc, gc))
    o = o.transpose(1, 0, 3, 2, 4).reshape(b, s, GLA_HEADS, GLA_DV).astype(x.dtype)
    o = rms_norm(o, g_head)
    o = o * jax.nn.silu(r.reshape(b, s, GLA_HEADS, GLA_DV))
    return o.reshape(b, s, GLA_V_DIM) @ w_out


def shared_kv(h, g, w_kv):
    b, s, _ = h.shape
    kv = rms_norm(h, g) @ w_kv
    k = kv[..., :SB_KV_DIM].reshape(b, s, SB_KV_HEADS, SB_HEAD_DIM).transpose(0, 2, 1, 3)
    v = kv[..., SB_KV_DIM:].reshape(b, s, SB_KV_HEADS, SB_HEAD_DIM).transpose(0, 2, 1, 3)
    return k, v


def stick_breaking_mixer(x, w_q, k_sh, v_sh, w_out):
    b, s, _ = x.shape
    n_blk = s // Q_BLOCK
    q = (x @ w_q).reshape(b, s, SB_KV_HEADS, SB_GROUP, SB_HEAD_DIM)
    q = q.reshape(b, n_blk, Q_BLOCK, SB_KV_HEADS, SB_GROUP, SB_HEAD_DIM).transpose(1, 0, 3, 4, 2, 5)
    key_pos = jnp.arange(s)
    scale = SB_HEAD_DIM ** -0.5

    def one_block(args):
        q_blk, blk = args
        q_pos = blk * Q_BLOCK + jnp.arange(Q_BLOCK)
        mask = key_pos[None, :] < q_pos[:, None]
        z = jnp.einsum('bngqd,bnsd->bngqs', q_blk, k_sh).astype(jnp.float32) * scale
        log_one_minus = jnp.where(mask, -jax.nn.softplus(z), 0.0)
        tail = lax.cumsum(log_one_minus, axis=4, reverse=True) - log_one_minus
        w = jnp.where(mask, jnp.exp(jax.nn.log_sigmoid(z) + tail), 0.0)
        return jnp.einsum('bngqs,bnsd->bngqd', w.astype(v_sh.dtype), v_sh)

    o = lax.map(one_block, (q, jnp.arange(n_blk)))
    o = o.transpose(1, 0, 4, 2, 3, 5).reshape(b, s, SB_HEADS * SB_HEAD_DIM)
    return o @ w_out


def setup_inputs(seed: int = 0) -> dict:
    key = jax.random.key(seed)
    ks = jax.random.split(key, 16)

    def nrm(k, shape, fan_in):
        return jax.random.normal(k, shape, jnp.float32) * (fan_in ** -0.5)

    def gain(k, shape):
        return 1.0 + 0.01 * jax.random.normal(k, shape, jnp.float32)

    return {
        "x": jax.random.normal(ks[0], (BATCH, SEQ, D_MODEL), jnp.float32),
        "norm1_g": gain(ks[1], (DEPTH, D_MODEL)),
        "norm2_g": gain(ks[2], (DEPTH, D_MODEL)),
        "gla_w_in": nrm(ks[3], (N_A_LAYERS, D_MODEL, GLA_IN_DIM), D_MODEL),
        "gla_w_gate_up": nrm(ks[4], (N_A_LAYERS, GLA_GATE_RANK, GLA_QK_DIM), GLA_GATE_RANK),
        "gla_b_gate": 0.1 * jax.random.normal(ks[5], (N_A_LAYERS, GLA_QK_DIM), jnp.float32),
        "gla_head_g": gain(ks[6], (N_A_LAYERS, GLA_DV)),
        "gla_w_out": nrm(ks[7], (N_A_LAYERS, GLA_V_DIM, D_MODEL), GLA_V_DIM),
        "kv_norm_g": gain(ks[8], (D_MODEL,)),
        "kv_w": nrm(ks[9], (D_MODEL, 2 * SB_KV_DIM), D_MODEL),
        "sb_w_q": nrm(ks[10], (N_B_LAYERS, D_MODEL, SB_HEADS * SB_HEAD_DIM), D_MODEL),
        "sb_w_out": nrm(ks[11], (N_B_LAYERS, SB_HEADS * SB_HEAD_DIM, D_MODEL), SB_HEADS * SB_HEAD_DIM),
        "mlp_w_up": nrm(ks[12], (DEPTH, D_MODEL, D_FF), D_MODEL),
        "mlp_w_down": nrm(ks[13], (DEPTH, D_FF, D_MODEL), D_FF),
        "final_g": gain(ks[14], (D_MODEL,)),
    }


def reference(x, norm1_g, norm2_g, gla_w_in, gla_w_gate_up, gla_b_gate, gla_head_g,
              gla_w_out, kv_norm_g, kv_w, sb_w_q, sb_w_out, mlp_w_up, mlp_w_down, final_g):
    h = x
    k_sh = None
    v_sh = None
    for layer in range(DEPTH):
        if layer < N_A_LAYERS:
            h = h + gla_mixer(rms_norm(h, norm1_g[layer]), gla_w_in[layer], gla_w_gate_up[layer],
                              gla_b_gate[layer], gla_head_g[layer], gla_w_out[layer])
        else:
            if layer == N_A_LAYERS:
                k_sh, v_sh = shared_kv(h, kv_norm_g, kv_w)
            j = layer - N_A_LAYERS
            h = h + stick_breaking_mixer(rms_norm(h, norm1_g[layer]), sb_w_q[j], k_sh, v_sh, sb_w_out[j])
        h = h + sq_relu_mlp(rms_norm(h, norm2_g[layer]), mlp_w_up[layer], mlp_w_down[layer])
    return rms_norm(h, final_g)
```

```python
import functools

import jax
import jax.numpy as jnp
from jax import lax
from jax.experimental import pallas as pl
from jax.experimental.pallas import tpu as pltpu

F32 = jnp.float32
BF16 = jnp.bfloat16

EPS = 1e-6
D_MODEL = 2048
D_FF = 4 * D_MODEL

GLA_HEADS = 4
GLA_DK = 256
GLA_DV = 512
GLA_QK = GLA_HEADS * GLA_DK
GLA_V = GLA_HEADS * GLA_DV
GLA_RANK = 16
GLA_TAU = 16.0
GLA_CHUNK = 128
GLA_SUB = 16

SB_HEADS = 16
SB_DH = 128
SB_KVH = 4
SB_GROUP = SB_HEADS // SB_KVH
SB_KV = SB_KVH * SB_DH
SB_TQ = 128
SB_TK = 128
SB_EXIT = -110.0

LANES = 128
VMEM_LIMIT = 56 * 1024 * 1024

NT_DIMS = (((1,), (1,)), ((), ()))


def _cparams(sem):
    return pltpu.CompilerParams(dimension_semantics=sem, vmem_limit_bytes=VMEM_LIMIT)


def _rms(x, g):
    return x * lax.rsqrt(jnp.mean(x * x, axis=-1, keepdims=True) + EPS) * g


def _split(x):
    hi = x.astype(BF16)
    lo = (x - hi.astype(F32)).astype(BF16)
    return hi, lo


def _dot(a, b):
    return jnp.dot(a, b, preferred_element_type=F32)


def _dot_nt(a, b):
    return lax.dot_general(a, b, NT_DIMS, preferred_element_type=F32)


def _dot_f32(a, b):
    ah, al = _split(a)
    bh, bl = _split(b)
    return _dot(ah, bh) + (_dot(ah, bl) + _dot(al, bh))


def _dot_f32_exact_rhs(a, b_bf16):
    ah, al = _split(a)
    return _dot(ah, b_bf16) + _dot(al, b_bf16)


def _norm_kernel(x_ref, g_ref, o_ref):
    o_ref[...] = _rms(x_ref[...], g_ref[...]).astype(o_ref.dtype)


def _norm(x, g, tm):
    t, d = x.shape
    return pl.pallas_call(
        _norm_kernel,
        grid=(t // tm,),
        in_specs=[pl.BlockSpec((tm, d), lambda i: (i, 0)),
                  pl.BlockSpec((1, d), lambda i: (0, 0))],
        out_specs=pl.BlockSpec((tm, d), lambda i: (i, 0)),
        out_shape=jax.ShapeDtypeStruct((t, d), BF16),
        compiler_params=_cparams(("parallel",)),
        name="rmsnorm",
    )(x, g.reshape(1, d))


def _mm_kernel(x_ref, w_ref, o_ref, *, nt, scale):
    if nt:
        acc = _dot_nt(w_ref[...], x_ref[...])
    else:
        acc = _dot(x_ref[...], w_ref[...])
    if scale != 1.0:
        acc = acc * scale
    o_ref[...] = acc.astype(o_ref.dtype)


def _mm(x, w, out_dtype, *, tm, tn, scale=1.0, name="proj"):
    t, k = x.shape
    n = w.shape[1]
    tn = min(tn, n)
    return pl.pallas_call(
        functools.partial(_mm_kernel, nt=False, scale=scale),
        grid=(t // tm, n // tn),
        in_specs=[pl.BlockSpec((tm, k), lambda i, j: (i, 0)),
                  pl.BlockSpec((k, tn), lambda i, j: (0, j))],
        out_specs=pl.BlockSpec((tm, tn), lambda i, j: (i, j)),
        out_shape=jax.ShapeDtypeStruct((t, n), out_dtype),
        compiler_params=_cparams(("parallel", "arbitrary")),
        name=name,
    )(x, w)


def _mm_t(x, wt, out_dtype, *, tm, tn, name="proj_t"):
    t, k = x.shape
    n = wt.shape[0]
    tn = min(tn, n)
    return pl.pallas_call(
        functools.partial(_mm_kernel, nt=True, scale=1.0),
        grid=(t // tm, n // tn),
        in_specs=[pl.BlockSpec((tm, k), lambda i, j: (i, 0)),
                  pl.BlockSpec((tn, k), lambda i, j: (j, 0))],
        out_specs=pl.BlockSpec((tn, tm), lambda i, j: (j, i)),
        out_shape=jax.ShapeDtypeStruct((n, t), out_dtype),
        compiler_params=_cparams(("parallel", "arbitrary")),
        name=name,
    )(x, wt)


def _mm_res_norm_kernel(y_ref, w_ref, res_ref, g_ref, h_ref, xn_ref):
    h = res_ref[...] + _dot(y_ref[...], w_ref[...])
    h_ref[...] = h
    xn_ref[...] = _rms(h, g_ref[...]).astype(xn_ref.dtype)


def _mm_res_norm(y, w, res, g, *, tm, name):
    t, k = y.shape
    n = w.shape[1]
    return pl.pallas_call(
        _mm_res_norm_kernel,
        grid=(t // tm,),
        in_specs=[pl.BlockSpec((tm, k), lambda i: (i, 0)),
                  pl.BlockSpec((k, n), lambda i: (0, 0)),
                  pl.BlockSpec((tm, n), lambda i: (i, 0)),
                  pl.BlockSpec((1, n), lambda i: (0, 0))],
        out_specs=[pl.BlockSpec((tm, n), lambda i: (i, 0)),
                   pl.BlockSpec((tm, n), lambda i: (i, 0))],
        out_shape=[jax.ShapeDtypeStruct((t, n), F32),
                   jax.ShapeDtypeStruct((t, n), BF16)],
        compiler_params=_cparams(("parallel",)),
        name=name,
    )(y, w, res, g.reshape(1, n))


def _mlp_kernel(xn_ref, h_ref, wu_ref, wd_ref, g1_ref, g2_ref, *rest, final):
    if final:
        out_ref, acc_ref = rest
    else:
        h_out_ref, xa_ref, xb_ref, acc_ref = rest
    f = pl.program_id(1)

    @pl.when(f == 0)
    def _():
        acc_ref[...] = h_ref[...]

    u = _dot(xn_ref[...], wu_ref[...])
    a = jnp.square(jnp.maximum(u, 0.0)).astype(BF16)
    acc_ref[...] += _dot(a, wd_ref[...])

    @pl.when(f == pl.num_programs(1) - 1)
    def _():
        h = acc_ref[...]
        if final:
            out_ref[...] = _rms(h, g1_ref[...])
        else:
            h_out_ref[...] = h
            r = lax.rsqrt(jnp.mean(h * h, axis=-1, keepdims=True) + EPS)
            hn = h * r
            xa_ref[...] = (hn * g1_ref[...]).astype(BF16)
            xb_ref[...] = (hn * g2_ref[...]).astype(BF16)


def _mlp(xn, h, wu, wd, g1, g2, *, tm, tf, final, name):
    t, d = xn.shape
    ff = wu.shape[1]
    row = pl.BlockSpec((tm, d), lambda i, f: (i, 0))
    vec = pl.BlockSpec((1, d), lambda i, f: (0, 0))
    if final:
        out_specs = row
        out_shape = jax.ShapeDtypeStruct((t, d), F32)
    else:
        out_specs = [row, row, row]
        out_shape = [jax.ShapeDtypeStruct((t, d), F32),
                     jax.ShapeDtypeStruct((t, d), BF16),
                     jax.ShapeDtypeStruct((t, d), BF16)]
    return pl.pallas_call(
        functools.partial(_mlp_kernel, final=final),
        grid=(t // tm, ff // tf),
        in_specs=[row, row,
                  pl.BlockSpec((d, tf), lambda i, f: (0, f)),
                  pl.BlockSpec((tf, d), lambda i, f: (f, 0)),
                  vec, vec],
        out_specs=out_specs,
        out_shape=out_shape,
        scratch_shapes=[pltpu.VMEM((tm, d), F32)],
        compiler_params=_cparams(("parallel", "arbitrary")),
        name=name,
    )(xn, h, wu, wd, g1.reshape(1, d), g2.reshape(1, d))


def _gla_kernel(q_ref, k_ref, vt_ref, r_ref, a_ref, wg_ref, bg_ref, gh_ref,
                y_ref, st_ref, cum_ref, p_ref):
    c = GLA_CHUNK
    s = GLA_SUB

    @pl.when(pl.program_id(0) == 0)
    def _():
        st_ref[...] = jnp.zeros_like(st_ref)

    z = _dot_f32(a_ref[...], wg_ref[...]) + bg_ref[...]
    log_alpha = (jnp.minimum(z, 0.0) - jnp.log1p(jnp.exp(-jnp.abs(z)))) * (1.0 / GLA_TAU)
    ri = lax.broadcasted_iota(jnp.int32, (c, c), 0)
    ci = lax.broadcasted_iota(jnp.int32, (c, c), 1)
    tri = (ci <= ri).astype(BF16)
    la_hi, la_lo = _split(log_alpha)
    cum_ref[...] = _dot(tri, la_hi) + _dot(tri, la_lo)

    row_k = lax.broadcasted_iota(jnp.int32, (c, GLA_DK), 0)
    row_s = lax.broadcasted_iota(jnp.int32, (s, GLA_DK), 0)
    lane_p = lax.broadcasted_iota(jnp.int32, (s, c), 1)

    for h in range(GLA_HEADS):
        ks = slice(h * GLA_DK, (h + 1) * GLA_DK)
        vs = slice(h * GLA_DV, (h + 1) * GLA_DV)
        ch = cum_ref[:, ks]
        kh = k_ref[:, ks]
        qs = q_ref[:, ks] * (GLA_DK ** -0.5)
        vt = vt_ref[vs, :]
        st = st_ref[h]

        qe = (qs * jnp.exp(ch)).astype(BF16)
        o = _dot_nt(qe, st.astype(BF16))

        def sub_block(ib, carry):
            r0 = pl.multiple_of(ib * s, s)
            c_i = cum_ref[pl.ds(r0, s), ks]
            q_i = q_ref[pl.ds(r0, s), ks] * (GLA_DK ** -0.5)
            ref_i = cum_ref[pl.ds(r0, 1), ks]
            q_sc = (q_i * jnp.exp(c_i - ref_i)).astype(BF16)
            k_sc = jnp.where(row_k < r0, kh * jnp.exp(jnp.minimum(ref_i - ch, 0.0)), 0.0)
            p_i = _dot_nt(q_sc, k_sc.astype(BF16))
            for j in range(s):
                c_j = cum_ref[pl.ds(r0 + j, 1), ks]
                k_j = k_ref[pl.ds(r0 + j, 1), ks]
                dec = jnp.exp(jnp.where(row_s >= j, c_i - c_j, -jnp.inf))
                col = jnp.sum(q_i * k_j * dec, axis=1, keepdims=True)
                p_i = jnp.where(lane_p == r0 + j, col, p_i)
            p_ref[pl.ds(r0, s), :] = p_i
            return carry

        lax.fori_loop(0, c // s, sub_block, 0)
        o = o + _dot_nt(p_ref[...].astype(BF16), vt)

        last = cum_ref[c - 1:c, ks]
        kd = (kh * jnp.exp(last - ch)).astype(BF16)
        st_ref[h] = st * jnp.exp(last) + _dot(vt, kd)

        on = _rms(o, gh_ref[...])
        rr = r_ref[:, vs].astype(F32)
        y_ref[:, vs] = (on * (rr * jax.nn.sigmoid(rr))).astype(y_ref.dtype)


def _gla(q, k, vt, r, a, wg, bg, gh):
    t = q.shape[0]
    c = GLA_CHUNK
    return pl.pallas_call(
        _gla_kernel,
        grid=(t // c,),
        in_specs=[pl.BlockSpec((c, GLA_QK), lambda i: (i, 0)),
                  pl.BlockSpec((c, GLA_QK), lambda i: (i, 0)),
                  pl.BlockSpec((GLA_V, c), lambda i: (0, i)),
                  pl.BlockSpec((c, GLA_V), lambda i: (i, 0)),
                  pl.BlockSpec((c, LANES), lambda i: (i, 0)),
                  pl.BlockSpec((LANES, GLA_QK), lambda i: (0, 0)),
                  pl.BlockSpec((1, GLA_QK), lambda i: (0, 0)),
                  pl.BlockSpec((1, GLA_DV), lambda i: (0, 0))],
        out_specs=pl.BlockSpec((c, GLA_V), lambda i: (i, 0)),
        out_shape=jax.ShapeDtypeStruct((t, GLA_V), BF16),
        scratch_shapes=[pltpu.VMEM((GLA_HEADS, GLA_DV, GLA_DK), F32),
                        pltpu.VMEM((c, GLA_QK), F32),
                        pltpu.VMEM((c, c), F32)],
        compiler_params=_cparams(("arbitrary",)),
        name="gla_scan",
    )(q, k, vt, r, a, wg, bg.reshape(1, GLA_QK), gh.reshape(1, GLA_DV))


def _sb_kernel(q_ref, k_ref, v_ref, o_ref, acc_ref, carry_ref):
    tq, tk = SB_TQ, SB_TK
    rows = SB_GROUP * tq
    qi = pl.program_id(1)
    qs = jnp.concatenate([q_ref[:, g * SB_DH:(g + 1) * SB_DH] for g in range(SB_GROUP)], axis=0)

    rj = lax.broadcasted_iota(jnp.int32, (tk, 2 * tk), 0)
    cj = lax.broadcasted_iota(jnp.int32, (tk, 2 * tk), 1)
    later = ((rj > cj) | (cj >= tk)).astype(BF16)

    def tile(kt, masked):
        k0 = pl.multiple_of(kt * tk, tk)
        z = _dot_nt(qs, k_ref[pl.ds(k0, tk), :])
        sp = jnp.maximum(z, 0.0) + jnp.log1p(jnp.exp(-jnp.abs(z)))
        lom = -sp
        if masked:
            qpos = lax.broadcasted_iota(jnp.int32, (rows, tk), 0) % tq
            kpos = lax.broadcasted_iota(jnp.int32, (rows, tk), 1)
            mask = kpos < qpos
            lom = jnp.where(mask, lom, 0.0)
        sums = _dot_f32_exact_rhs(lom, later)
        tail = sums[:, :tk] + carry_ref[...]
        w = jnp.exp((z - sp) + tail)
        if masked:
            w = jnp.where(mask, w, 0.0)
        acc_ref[...] += _dot(w.astype(BF16), v_ref[pl.ds(k0, tk), :])
        carry = carry_ref[...] + sums[:, tk:]
        carry_ref[...] = carry
        return jnp.max(carry)

    acc_ref[...] = jnp.zeros_like(acc_ref)
    carry_ref[...] = jnp.zeros_like(carry_ref)
    top = tile(qi, True)

    def cond(state):
        kt, mx = state
        return jnp.logical_and(kt >= 0, mx > SB_EXIT)

    def body(state):
        kt, _ = state
        return kt - 1, tile(kt, False)

    lax.while_loop(cond, body, (qi - 1, top))

    for g in range(SB_GROUP):
        o_ref[:, g * SB_DH:(g + 1) * SB_DH] = acc_ref[g * tq:(g + 1) * tq, :].astype(o_ref.dtype)


def _sb_attention(q, kv):
    t = q.shape[0]
    tq = SB_TQ
    gw = SB_GROUP * SB_DH
    return pl.pallas_call(
        _sb_kernel,
        grid=(SB_KVH, t // tq),
        in_specs=[pl.BlockSpec((tq, gw), lambda n, i: (i, n)),
                  pl.BlockSpec((t, SB_DH), lambda n, i: (0, n)),
                  pl.BlockSpec((t, SB_DH), lambda n, i: (0, SB_KVH + n))],
        out_specs=pl.BlockSpec((tq, gw), lambda n, i: (i, n)),
        out_shape=jax.ShapeDtypeStruct((t, SB_HEADS * SB_DH), BF16),
        scratch_shapes=[pltpu.VMEM((SB_GROUP * tq, SB_DH), F32),
                        pltpu.VMEM((SB_GROUP * tq, SB_TK), F32)],
        compiler_params=_cparams(("arbitrary", "arbitrary")),
        name="stick_breaking",
    )(q, kv, kv)


def _row_tile(t, want):
    return want if t % want == 0 else t


def kernel(x, norm1_g, norm2_g, gla_w_in, gla_w_gate_up, gla_b_gate, gla_head_g, gla_w_out,
           kv_norm_g, kv_w, sb_w_q, sb_w_out, mlp_w_up, mlp_w_down, final_g):
    b, t, d = x.shape
    assert b == 1 and d == D_MODEL and t % GLA_CHUNK == 0
    x2 = x.reshape(t, d)
    tm_big = _row_tile(t, 1024)
    tm = _row_tile(t, 512)

    w_in = gla_w_in[0]
    o1 = GLA_QK
    o2 = 2 * GLA_QK
    o3 = o2 + GLA_V
    o4 = o3 + GLA_V
    w_q = w_in[:, :o1].astype(BF16)
    w_k = w_in[:, o1:o2].astype(BF16)
    w_vt = w_in[:, o2:o3].T.astype(BF16)
    w_r = w_in[:, o3:o4].astype(BF16)
    w_a = jnp.pad(w_in[:, o4:], ((0, 0), (0, LANES - GLA_RANK))).astype(BF16)
    w_g = jnp.pad(gla_w_gate_up[0], ((0, LANES - GLA_RANK), (0, 0)))

    xn = _norm(x2, norm1_g[0], tm)
    q = _mm(xn, w_q, F32, tm=tm_big, tn=512, name="gla_q")
    k = _mm(xn, w_k, F32, tm=tm_big, tn=512, name="gla_k")
    vt = _mm_t(xn, w_vt, BF16, tm=tm_big, tn=512, name="gla_vt")
    r = _mm(xn, w_r, BF16, tm=tm_big, tn=512, name="gla_r")
    a = _mm(xn, w_a, F32, tm=tm_big, tn=LANES, name="gla_a")
    y = _gla(q, k, vt, r, a, w_g, gla_b_gate[0], gla_head_g[0])
    h, xn = _mm_res_norm(y, gla_w_out[0].astype(BF16), x2, norm2_g[0], tm=tm, name="gla_out")
    h, xn_q, xn_kv = _mlp(xn, h, mlp_w_up[0].astype(BF16), mlp_w_down[0].astype(BF16),
                          norm1_g[1], kv_norm_g, tm=tm, tf=512, final=False, name="mlp0")

    q2 = _mm(xn_q, sb_w_q[0].astype(BF16), BF16, tm=tm_big, tn=512,
             scale=SB_DH ** -0.5, name="sb_q")
    kv = _mm(xn_kv, kv_w.astype(BF16), BF16, tm=tm_big, tn=512, name="shared_kv")
    o = _sb_attention(q2, kv)
    h, xn = _mm_res_norm(o, sb_w_out[0].astype(BF16), h, norm2_g[1], tm=tm, name="sb_out")
    out = _mlp(xn, h, mlp_w_up[1].astype(BF16), mlp_w_down[1].astype(BF16),
               final_g, final_g, tm=tm, tf=512, final=True, name="mlp1")
    return out.reshape(b, t, d)
```

```python
import functools

import jax
import jax.numpy as jnp
from jax import lax
from jax.experimental import pallas as pl
from jax.experimental.pallas import tpu as pltpu

F32 = jnp.float32
BF16 = jnp.bfloat16

EPS = 1e-6
D_MODEL = 2048
D_FF = 4 * D_MODEL

GLA_HEADS = 4
GLA_DK = 256
GLA_DV = 512
GLA_QK = GLA_HEADS * GLA_DK
GLA_V = GLA_HEADS * GLA_DV
GLA_RANK = 16
GLA_TAU = 16.0
GLA_CHUNK = 128
GLA_DIAG = 8
GLA_LEVELS = (64, 32, 16, 8)

SB_HEADS = 16
SB_DH = 128
SB_KVH = 4
SB_GROUP = SB_HEADS // SB_KVH
SB_KV = SB_KVH * SB_DH
SB_TQ = 128
SB_TK = 128
SB_FUSED = 3
SB_EXIT = -110.0

LANES = 128
VMEM_LIMIT = 56 * 1024 * 1024

NT_DIMS = (((1,), (1,)), ((), ()))


def _cparams(sem):
    return pltpu.CompilerParams(dimension_semantics=sem, vmem_limit_bytes=VMEM_LIMIT)


def _rms(x, g):
    return x * lax.rsqrt(jnp.mean(x * x, axis=-1, keepdims=True) + EPS) * g


def _split(x):
    hi = x.astype(BF16)
    lo = (x - hi.astype(F32)).astype(BF16)
    return hi, lo


def _dot(a, b):
    return jnp.dot(a, b, preferred_element_type=F32)


def _dot_nt(a, b):
    return lax.dot_general(a, b, NT_DIMS, preferred_element_type=F32)


def _log1p_exp_neg_abs(z):
    return jnp.log(1.0 + jnp.exp(-jnp.abs(z)))


def _norm_kernel(x_ref, g_ref, o_ref):
    o_ref[...] = _rms(x_ref[...], g_ref[...]).astype(o_ref.dtype)


def _norm(x, g, tm):
    t, d = x.shape
    return pl.pallas_call(
        _norm_kernel,
        grid=(t // tm,),
        in_specs=[pl.BlockSpec((tm, d), lambda i: (i, 0)),
                  pl.BlockSpec((1, d), lambda i: (0, 0))],
        out_specs=pl.BlockSpec((tm, d), lambda i: (i, 0)),
        out_shape=jax.ShapeDtypeStruct((t, d), BF16),
        compiler_params=_cparams(("parallel",)),
        name="rmsnorm",
    )(x, g.reshape(1, d))


def _mm_kernel(x_ref, w_ref, o_ref, *, nt, scale):
    if nt:
        acc = _dot_nt(w_ref[...], x_ref[...])
    else:
        acc = _dot(x_ref[...], w_ref[...])
    if scale != 1.0:
        acc = acc * scale
    o_ref[...] = acc.astype(o_ref.dtype)


def _mm(x, w, out_dtype, *, tm, tn, scale=1.0, name="proj"):
    t, k = x.shape
    n = w.shape[1]
    tn = min(tn, n)
    return pl.pallas_call(
        functools.partial(_mm_kernel, nt=False, scale=scale),
        grid=(t // tm, n // tn),
        in_specs=[pl.BlockSpec((tm, k), lambda i, j: (i, 0)),
                  pl.BlockSpec((k, tn), lambda i, j: (0, j))],
        out_specs=pl.BlockSpec((tm, tn), lambda i, j: (i, j)),
        out_shape=jax.ShapeDtypeStruct((t, n), out_dtype),
        compiler_params=_cparams(("parallel", "arbitrary")),
        name=name,
    )(x, w)


def _mm_t(x, wt, out_dtype, *, tm, tn, name="proj_t"):
    t, k = x.shape
    n = wt.shape[0]
    tn = min(tn, n)
    return pl.pallas_call(
        functools.partial(_mm_kernel, nt=True, scale=1.0),
        grid=(t // tm, n // tn),
        in_specs=[pl.BlockSpec((tm, k), lambda i, j: (i, 0)),
                  pl.BlockSpec((tn, k), lambda i, j: (j, 0))],
        out_specs=pl.BlockSpec((tn, tm), lambda i, j: (j, i)),
        out_shape=jax.ShapeDtypeStruct((n, t), out_dtype),
        compiler_params=_cparams(("parallel", "arbitrary")),
        name=name,
    )(x, wt)


def _mm_res_norm_kernel(y_ref, w_ref, res_ref, g_ref, h_ref, xn_ref):
    h = res_ref[...] + _dot(y_ref[...], w_ref[...])
    h_ref[...] = h
    xn_ref[...] = _rms(h, g_ref[...]).astype(xn_ref.dtype)


def _mm_res_norm(y, w, res, g, *, tm, name):
    t, k = y.shape
    n = w.shape[1]
    return pl.pallas_call(
        _mm_res_norm_kernel,
        grid=(t // tm,),
        in_specs=[pl.BlockSpec((tm, k), lambda i: (i, 0)),
                  pl.BlockSpec((k, n), lambda i: (0, 0)),
                  pl.BlockSpec((tm, n), lambda i: (i, 0)),
                  pl.BlockSpec((1, n), lambda i: (0, 0))],
        out_specs=[pl.BlockSpec((tm, n), lambda i: (i, 0)),
                   pl.BlockSpec((tm, n), lambda i: (i, 0))],
        out_shape=[jax.ShapeDtypeStruct((t, n), F32),
                   jax.ShapeDtypeStruct((t, n), BF16)],
        compiler_params=_cparams(("parallel",)),
        name=name,
    )(y, w, res, g.reshape(1, n))


def _mlp_kernel(xn_ref, h_ref, wu_ref, wd_ref, g1_ref, g2_ref, *rest, final):
    if final:
        out_ref, acc_ref = rest
    else:
        h_out_ref, xa_ref, xb_ref, acc_ref = rest
    f = pl.program_id(1)

    @pl.when(f == 0)
    def _():
        acc_ref[...] = h_ref[...]

    u = _dot(xn_ref[...], wu_ref[...])
    a = jnp.square(jnp.maximum(u, 0.0)).astype(BF16)
    acc_ref[...] += _dot(a, wd_ref[...])

    @pl.when(f == pl.num_programs(1) - 1)
    def _():
        h = acc_ref[...]
        if final:
            out_ref[...] = _rms(h, g1_ref[...])
        else:
            h_out_ref[...] = h
            r = lax.rsqrt(jnp.mean(h * h, axis=-1, keepdims=True) + EPS)
            hn = h * r
            xa_ref[...] = (hn * g1_ref[...]).astype(BF16)
            xb_ref[...] = (hn * g2_ref[...]).astype(BF16)


def _mlp(xn, h, wu, wd, g1, g2, *, tm, tf, final, name):
    t, d = xn.shape
    ff = wu.shape[1]
    row = pl.BlockSpec((tm, d), lambda i, f: (i, 0))
    vec = pl.BlockSpec((1, d), lambda i, f: (0, 0))
    if final:
        out_specs = row
        out_shape = jax.ShapeDtypeStruct((t, d), F32)
    else:
        out_specs = [row, row, row]
        out_shape = [jax.ShapeDtypeStruct((t, d), F32),
                     jax.ShapeDtypeStruct((t, d), BF16),
                     jax.ShapeDtypeStruct((t, d), BF16)]
    return pl.pallas_call(
        functools.partial(_mlp_kernel, final=final),
        grid=(t // tm, ff // tf),
        in_specs=[row, row,
                  pl.BlockSpec((d, tf), lambda i, f: (0, f)),
                  pl.BlockSpec((tf, d), lambda i, f: (f, 0)),
                  vec, vec],
        out_specs=out_specs,
        out_shape=out_shape,
        scratch_shapes=[pltpu.VMEM((tm, d), F32)],
        compiler_params=_cparams(("parallel", "arbitrary")),
        name=name,
    )(xn, h, wu, wd, g1.reshape(1, d), g2.reshape(1, d))


def _gla_kernel(q_ref, k_ref, vt_ref, r_ref, a_ref, wgh_ref, wgl_ref, bg_ref, gh_ref,
                y_ref, st_ref, cum_ref):
    c = GLA_CHUNK
    nd = GLA_DIAG

    @pl.when(pl.program_id(0) == 0)
    def _():
        st_ref[...] = jnp.zeros_like(st_ref)

    a_hi, a_lo = _split(a_ref[...])
    z = (_dot(a_hi, wgh_ref[...]) + (_dot(a_hi, wgl_ref[...]) + _dot(a_lo, wgh_ref[...]))
         + bg_ref[...])
    log_alpha = (jnp.minimum(z, 0.0) - _log1p_exp_neg_abs(z)) * (1.0 / GLA_TAU)
    ri = lax.broadcasted_iota(jnp.int32, (c, c), 0)
    ci = lax.broadcasted_iota(jnp.int32, (c, c), 1)
    tri = (ci <= ri).astype(BF16)
    la_hi, la_lo = _split(log_alpha)
    cum_ref[...] = _dot(tri, la_hi) + _dot(tri, la_lo)

    row_k = lax.broadcasted_iota(jnp.int32, (c, GLA_DK), 0)
    row_d = lax.broadcasted_iota(jnp.int32, (nd, GLA_DK), 0)
    lane_d = lax.broadcasted_iota(jnp.int32, (nd, c), 1)
    upper = [(row_k % (2 * b)) >= b for b in GLA_LEVELS]
    same_group = [(ri // (2 * b)) == (ci // (2 * b)) for b in GLA_LEVELS]

    for h in range(GLA_HEADS):
        ks = slice(h * GLA_DK, (h + 1) * GLA_DK)
        vs = slice(h * GLA_DV, (h + 1) * GLA_DV)
        ch = cum_ref[:, ks]
        kh = k_ref[:, ks]
        qs = q_ref[:, ks] * (GLA_DK ** -0.5)
        vt = vt_ref[vs, :]
        st = st_ref[h]

        qe = (qs * jnp.exp(ch)).astype(BF16)
        o = _dot_nt(qe, st.astype(BF16))

        p = jnp.zeros((c, c), F32)
        for lvl, b in enumerate(GLA_LEVELS):
            ref = jnp.concatenate(
                [jnp.broadcast_to(cum_ref[g + b:g + b + 1, ks], (2 * b, GLA_DK))
                 for g in range(0, c, 2 * b)], axis=0)
            e = jnp.exp(-jnp.abs(ch - ref))
            q_sc = jnp.where(upper[lvl], qs * e, 0.0).astype(BF16)
            k_sc = jnp.where(upper[lvl], 0.0, kh * e).astype(BF16)
            p = p + jnp.where(same_group[lvl], _dot_nt(q_sc, k_sc), 0.0)

        p_rows = []
        for r0 in range(0, c, nd):
            c_i = ch[r0:r0 + nd]
            q_i = qs[r0:r0 + nd]
            p_i = p[r0:r0 + nd]
            for j in range(nd):
                c_j = cum_ref[r0 + j:r0 + j + 1, ks]
                k_j = k_ref[r0 + j:r0 + j + 1, ks]
                dec = jnp.exp(jnp.where(row_d >= j, c_i - c_j, -jnp.inf))
                col = jnp.sum(q_i * k_j * dec, axis=1, keepdims=True)
                p_i = jnp.where(lane_d == r0 + j, col, p_i)
            p_rows.append(p_i)
        p = jnp.concatenate(p_rows, axis=0)
        o = o + _dot_nt(p.astype(BF16), vt)

        last = cum_ref[c - 1:c, ks]
        kd = (kh * jnp.exp(last - ch)).astype(BF16)
        st_ref[h] = st * jnp.exp(last) + _dot(vt, kd)

        on = _rms(o, gh_ref[...])
        rr = r_ref[:, vs].astype(F32)
        y_ref[:, vs] = (on * (rr * jax.nn.sigmoid(rr))).astype(y_ref.dtype)


def _gla(q, k, vt, r, a, wg, bg, gh):
    t = q.shape[0]
    c = GLA_CHUNK
    wg_hi, wg_lo = _split(wg)
    const = lambda shape: pl.BlockSpec(shape, lambda i: (0, 0))
    return pl.pallas_call(
        _gla_kernel,
        grid=(t // c,),
        in_specs=[pl.BlockSpec((c, GLA_QK), lambda i: (i, 0)),
                  pl.BlockSpec((c, GLA_QK), lambda i: (i, 0)),
                  pl.BlockSpec((GLA_V, c), lambda i: (0, i)),
                  pl.BlockSpec((c, GLA_V), lambda i: (i, 0)),
                  pl.BlockSpec((c, LANES), lambda i: (i, 0)),
                  const((LANES, GLA_QK)),
                  const((LANES, GLA_QK)),
                  const((1, GLA_QK)),
                  const((1, GLA_DV))],
        out_specs=pl.BlockSpec((c, GLA_V), lambda i: (i, 0)),
        out_shape=jax.ShapeDtypeStruct((t, GLA_V), BF16),
        scratch_shapes=[pltpu.VMEM((GLA_HEADS, GLA_DV, GLA_DK), F32),
                        pltpu.VMEM((c, GLA_QK), F32)],
        compiler_params=_cparams(("arbitrary",)),
        name="gla_scan",
    )(q, k, vt, r, a, wg_hi, wg_lo, bg.reshape(1, GLA_QK), gh.reshape(1, GLA_DV))


def _sb_sum_matrix():
    tk = SB_TK
    n = SB_FUSED * tk
    j = jnp.arange(n)[:, None]
    s = jnp.arange(n)[None, :]
    later = (j > s)
    cols = [later[:, :tk], jnp.ones((n, tk), bool)] + [later[:, i * tk:(i + 1) * tk] for i in range(1, SB_FUSED)]
    return -jnp.concatenate(cols, axis=1).astype(BF16)


def _sb_kernel(q_ref, k_ref, v_ref, u_ref, o_ref, acc_ref, carry_ref):
    tq, tk = SB_TQ, SB_TK
    rows = SB_GROUP * tq
    nf = SB_FUSED
    qi = pl.program_id(1)
    qs = jnp.concatenate([q_ref[:, g * SB_DH:(g + 1) * SB_DH] for g in range(SB_GROUP)], axis=0)
    qpos = lax.broadcasted_iota(jnp.int32, (rows, tk), 0) % tq
    kpos = lax.broadcasted_iota(jnp.int32, (rows, tk), 1)
    causal = kpos < qpos

    def tile(kt, masked):
        k0 = pl.multiple_of(kt * tk, tk)
        z = _dot_nt(qs, k_ref[pl.ds(k0, tk), :])
        lse = _log1p_exp_neg_abs(z)
        sp = jnp.maximum(z, 0.0) + lse
        spm = jnp.where(causal, sp, 0.0) if masked else sp
        hi, lo = _split(spm)
        u_tot = u_ref[(nf - 1) * tk:, tk:2 * tk]
        u_in = u_ref[(nf - 1) * tk:, nf * tk:]
        tot = _dot(hi, u_tot) + _dot(lo, u_tot)
        tail = (_dot(hi, u_in) + _dot(lo, u_in)) + carry_ref[...]
        w = jnp.exp((z - sp) + tail)
        if masked:
            w = jnp.where(causal, w, 0.0)
        acc_ref[...] += _dot(w.astype(BF16), v_ref[pl.ds(k0, tk), :])
        carry = carry_ref[...] + tot
        carry_ref[...] = carry
        return jnp.max(carry)

    def first_fused():
        k0 = pl.multiple_of((qi - (nf - 1)) * tk, tk)
        z = _dot_nt(qs, k_ref[pl.ds(k0, nf * tk), :])
        lse = _log1p_exp_neg_abs(z)
        sp = jnp.maximum(z, 0.0) + lse
        spm = jnp.concatenate([sp[:, :(nf - 1) * tk],
                               jnp.where(causal, sp[:, (nf - 1) * tk:], 0.0)], axis=1)
        hi, lo = _split(spm)
        first = _dot(hi, u_ref[:, :2 * tk]) + _dot(lo, u_ref[:, :2 * tk])
        tails = [first[:, :tk]]
        for i in range(1, nf):
            u_i = u_ref[i * tk:, (i + 1) * tk:(i + 2) * tk]
            tails.append(_dot(hi[:, i * tk:], u_i) + _dot(lo[:, i * tk:], u_i))
        tot = first[:, tk:]
        w = jnp.exp((z - sp) + jnp.concatenate(tails, axis=1))
        w = jnp.concatenate([w[:, :(nf - 1) * tk],
                             jnp.where(causal, w[:, (nf - 1) * tk:], 0.0)], axis=1)
        acc_ref[...] = _dot(w.astype(BF16), v_ref[pl.ds(k0, nf * tk), :])
        carry_ref[...] = tot
        return qi - nf, jnp.max(tot)

    def first_single():
        acc_ref[...] = jnp.zeros_like(acc_ref)
        carry_ref[...] = jnp.zeros_like(carry_ref)
        return qi - 1, tile(qi, True)

    start = lax.cond(qi >= nf - 1, first_fused, first_single)

    def cond(state):
        kt, mx = state
        return jnp.logical_and(kt >= 0, mx > SB_EXIT)

    def body(state):
        kt, _ = state
        return kt - 1, tile(kt, False)

    lax.while_loop(cond, body, start)

    for g in range(SB_GROUP):
        o_ref[:, g * SB_DH:(g + 1) * SB_DH] = acc_ref[g * tq:(g + 1) * tq, :].astype(o_ref.dtype)


def _sb_attention(q, kv):
    t = q.shape[0]
    tq = SB_TQ
    gw = SB_GROUP * SB_DH
    u = _sb_sum_matrix()
    return pl.pallas_call(
        _sb_kernel,
        grid=(SB_KVH, t // tq),
        in_specs=[pl.BlockSpec((tq, gw), lambda n, i: (i, n)),
                  pl.BlockSpec((t, SB_DH), lambda n, i: (0, n)),
                  pl.BlockSpec((t, SB_DH), lambda n, i: (0, SB_KVH + n)),
                  pl.BlockSpec(u.shape, lambda n, i: (0, 0))],
        out_specs=pl.BlockSpec((tq, gw), lambda n, i: (i, n)),
        out_shape=jax.ShapeDtypeStruct((t, SB_HEADS * SB_DH), BF16),
        scratch_shapes=[pltpu.VMEM((SB_GROUP * tq, SB_DH), F32),
                        pltpu.VMEM((SB_GROUP * tq, SB_TK), F32)],
        compiler_params=_cparams(("arbitrary", "arbitrary")),
        name="stick_breaking",
    )(q, kv, kv, u)


def _row_tile(t, want):
    return want if t % want == 0 else t


def kernel(x, norm1_g, norm2_g, gla_w_in, gla_w_gate_up, gla_b_gate, gla_head_g, gla_w_out,
           kv_norm_g, kv_w, sb_w_q, sb_w_out, mlp_w_up, mlp_w_down, final_g):
    b, t, d = x.shape
    assert b == 1 and d == D_MODEL and t % GLA_CHUNK == 0
    x2 = x.reshape(t, d)
    tm_big = _row_tile(t, 1024)
    tm = _row_tile(t, 512)

    w_in = gla_w_in[0]
    o1 = GLA_QK
    o2 = 2 * GLA_QK
    o3 = o2 + GLA_V
    o4 = o3 + GLA_V
    w_q = w_in[:, :o1].astype(BF16)
    w_k = w_in[:, o1:o2].astype(BF16)
    w_vt = w_in[:, o2:o3].T.astype(BF16)
    w_r = w_in[:, o3:o4].astype(BF16)
    w_a = jnp.pad(w_in[:, o4:], ((0, 0), (0, LANES - GLA_RANK))).astype(BF16)
    w_g = jnp.pad(gla_w_gate_up[0], ((0, LANES - GLA_RANK), (0, 0)))

    xn = _norm(x2, norm1_g[0], tm)
    q = _mm(xn, w_q, F32, tm=tm_big, tn=512, name="gla_q")
    k = _mm(xn, w_k, F32, tm=tm_big, tn=512, name="gla_k")
    vt = _mm_t(xn, w_vt, BF16, tm=tm_big, tn=512, name="gla_vt")
    r = _mm(xn, w_r, BF16, tm=tm_big, tn=512, name="gla_r")
    a = _mm(xn, w_a, F32, tm=tm_big, tn=LANES, name="gla_a")
    y = _gla(q, k, vt, r, a, w_g, gla_b_gate[0], gla_head_g[0])
    h, xn = _mm_res_norm(y, gla_w_out[0].astype(BF16), x2, norm2_g[0], tm=tm, name="gla_out")
    h, xn_q, xn_kv = _mlp(xn, h, mlp_w_up[0].astype(BF16), mlp_w_down[0].astype(BF16),
                          norm1_g[1], kv_norm_g, tm=tm, tf=512, final=False, name="mlp0")

    q2 = _mm(xn_q, sb_w_q[0].astype(BF16), BF16, tm=tm_big, tn=512,
             scale=SB_DH ** -0.5, name="sb_q")
    kv = _mm(xn_kv, kv_w.astype(BF16), BF16, tm=tm_big, tn=512, name="shared_kv")
    o = _sb_attention(q2, kv)
    h, xn = _mm_res_norm(o, sb_w_out[0].astype(BF16), h, norm2_g[1], tm=tm, name="sb_out")
    out = _mlp(xn, h, mlp_w_up[1].astype(BF16), mlp_w_down[1].astype(BF16),
               final_g, final_g, tm=tm, tf=512, final=True, name="mlp1")
    return out.reshape(b, t, d)
```

```python
import functools

import jax
import jax.numpy as jnp
from jax import lax
from jax.experimental import pallas as pl
from jax.experimental.pallas import tpu as pltpu

F32 = jnp.float32
BF16 = jnp.bfloat16

EPS = 1e-6
D_MODEL = 2048
D_FF = 4 * D_MODEL

GLA_HEADS = 4
GLA_DK = 256
GLA_DV = 512
GLA_QK = GLA_HEADS * GLA_DK
GLA_V = GLA_HEADS * GLA_DV
GLA_RANK = 16
GLA_TAU = 16.0
GLA_CHUNK = 128
GLA_DIAG = 8
GLA_LEVELS = (64, 32, 16, 8)

SB_HEADS = 16
SB_DH = 128
SB_KVH = 4
SB_GROUP = SB_HEADS // SB_KVH
SB_KV = SB_KVH * SB_DH
SB_TQ = 128
SB_TK = 128
SB_FUSED = 3
SB_EXIT = -110.0

LANES = 128
VMEM_LIMIT = 56 * 1024 * 1024

NT_DIMS = (((1,), (1,)), ((), ()))


def _cparams(sem):
    return pltpu.CompilerParams(dimension_semantics=sem, vmem_limit_bytes=VMEM_LIMIT)


def _rms(x, g):
    return x * lax.rsqrt(jnp.mean(x * x, axis=-1, keepdims=True) + EPS) * g


def _split(x):
    hi = x.astype(BF16)
    lo = (x - hi.astype(F32)).astype(BF16)
    return hi, lo


def _dot(a, b):
    return jnp.dot(a, b, preferred_element_type=F32)


def _dot_nt(a, b):
    return lax.dot_general(a, b, NT_DIMS, preferred_element_type=F32)


def _log1p_exp_neg_abs(z):
    return jnp.log(1.0 + jnp.exp(-jnp.abs(z)))


def _norm_kernel(x_ref, g_ref, o_ref):
    o_ref[...] = _rms(x_ref[...], g_ref[...]).astype(o_ref.dtype)


def _norm(x, g, tm):
    t, d = x.shape
    return pl.pallas_call(
        _norm_kernel,
        grid=(t // tm,),
        in_specs=[pl.BlockSpec((tm, d), lambda i: (i, 0)),
                  pl.BlockSpec((1, d), lambda i: (0, 0))],
        out_specs=pl.BlockSpec((tm, d), lambda i: (i, 0)),
        out_shape=jax.ShapeDtypeStruct((t, d), BF16),
        compiler_params=_cparams(("parallel",)),
        name="rmsnorm",
    )(x, g.reshape(1, d))


def _mm_kernel(x_ref, w_ref, o_ref, *, nt, scale):
    if nt:
        acc = _dot_nt(w_ref[...], x_ref[...])
    else:
        acc = _dot(x_ref[...], w_ref[...])
    if scale != 1.0:
        acc = acc * scale
    o_ref[...] = acc.astype(o_ref.dtype)


def _mm(x, w, out_dtype, *, tm, tn, scale=1.0, name="proj"):
    t, k = x.shape
    n = w.shape[1]
    tn = min(tn, n)
    return pl.pallas_call(
        functools.partial(_mm_kernel, nt=False, scale=scale),
        grid=(t // tm, n // tn),
        in_specs=[pl.BlockSpec((tm, k), lambda i, j: (i, 0)),
                  pl.BlockSpec((k, tn), lambda i, j: (0, j))],
        out_specs=pl.BlockSpec((tm, tn), lambda i, j: (i, j)),
        out_shape=jax.ShapeDtypeStruct((t, n), out_dtype),
        compiler_params=_cparams(("parallel", "arbitrary")),
        name=name,
    )(x, w)


def _mm_t(x, wt, out_dtype, *, tm, tn, name="proj_t"):
    t, k = x.shape
    n = wt.shape[0]
    tn = min(tn, n)
    return pl.pallas_call(
        functools.partial(_mm_kernel, nt=True, scale=1.0),
        grid=(t // tm, n // tn),
        in_specs=[pl.BlockSpec((tm, k), lambda i, j: (i, 0)),
                  pl.BlockSpec((tn, k), lambda i, j: (j, 0))],
        out_specs=pl.BlockSpec((tn, tm), lambda i, j: (j, i)),
        out_shape=jax.ShapeDtypeStruct((n, t), out_dtype),
        compiler_params=_cparams(("parallel", "arbitrary")),
        name=name,
    )(x, wt)


def _mm_res_norm_kernel(y_ref, w_ref, res_ref, g_ref, h_ref, xn_ref):
    h = res_ref[...] + _dot(y_ref[...], w_ref[...])
    h_ref[...] = h
    xn_ref[...] = _rms(h, g_ref[...]).astype(xn_ref.dtype)


def _mm_res_norm(y, w, res, g, *, tm, name):
    t, k = y.shape
    n = w.shape[1]
    return pl.pallas_call(
        _mm_res_norm_kernel,
        grid=(t // tm,),
        in_specs=[pl.BlockSpec((tm, k), lambda i: (i, 0)),
                  pl.BlockSpec((k, n), lambda i: (0, 0)),
                  pl.BlockSpec((tm, n), lambda i: (i, 0)),
                  pl.BlockSpec((1, n), lambda i: (0, 0))],
        out_specs=[pl.BlockSpec((tm, n), lambda i: (i, 0)),
                   pl.BlockSpec((tm, n), lambda i: (i, 0))],
        out_shape=[jax.ShapeDtypeStruct((t, n), F32),
                   jax.ShapeDtypeStruct((t, n), BF16)],
        compiler_params=_cparams(("parallel",)),
        name=name,
    )(y, w, res, g.reshape(1, n))


def _mlp_kernel(xn_ref, h_ref, wu_ref, wd_ref, g1_ref, g2_ref, acc_ref, *norm_refs, final):
    f = pl.program_id(1)

    @pl.when(f == 0)
    def _():
        acc_ref[...] = h_ref[...]

    u = _dot(xn_ref[...], wu_ref[...])
    a = jnp.square(jnp.maximum(u, 0.0)).astype(BF16)
    acc_ref[...] += _dot(a, wd_ref[...])

    @pl.when(f == pl.num_programs(1) - 1)
    def _():
        h = acc_ref[...]
        hn = h * lax.rsqrt(jnp.mean(h * h, axis=-1, keepdims=True) + EPS)
        if final:
            acc_ref[...] = hn * g1_ref[...]
        else:
            xa_ref, xb_ref = norm_refs
            xa_ref[...] = (hn * g1_ref[...]).astype(BF16)
            xb_ref[...] = (hn * g2_ref[...]).astype(BF16)


def _mlp(xn, h, wu, wd, g1, g2, *, tm, tf, final, name):
    t, d = xn.shape
    ff = wu.shape[1]
    row = pl.BlockSpec((tm, d), lambda i, f: (i, 0))
    vec = pl.BlockSpec((1, d), lambda i, f: (0, 0))
    if final:
        out_specs = row
        out_shape = jax.ShapeDtypeStruct((t, d), F32)
    else:
        out_specs = [row, row, row]
        out_shape = [jax.ShapeDtypeStruct((t, d), F32),
                     jax.ShapeDtypeStruct((t, d), BF16),
                     jax.ShapeDtypeStruct((t, d), BF16)]
    return pl.pallas_call(
        functools.partial(_mlp_kernel, final=final),
        grid=(t // tm, ff // tf),
        in_specs=[row, row,
                  pl.BlockSpec((d, tf), lambda i, f: (0, f)),
                  pl.BlockSpec((tf, d), lambda i, f: (f, 0)),
                  vec, vec],
        out_specs=out_specs,
        out_shape=out_shape,
        compiler_params=_cparams(("parallel", "arbitrary")),
        name=name,
    )(xn, h, wu, wd, g1.reshape(1, d), g2.reshape(1, d))


def _gla_kernel(q_ref, k_ref, vt_ref, r_ref, a_ref, wgh_ref, wgl_ref, bg_ref, gh_ref,
                y_ref, st_ref, cum_ref):
    c = GLA_CHUNK
    nd = GLA_DIAG

    @pl.when(pl.program_id(0) == 0)
    def _():
        st_ref[...] = jnp.zeros_like(st_ref)

    a_hi, a_lo = _split(a_ref[...])
    z = (_dot(a_hi, wgh_ref[...]) + (_dot(a_hi, wgl_ref[...]) + _dot(a_lo, wgh_ref[...]))
         + bg_ref[...])
    log_alpha = (jnp.minimum(z, 0.0) - _log1p_exp_neg_abs(z)) * (1.0 / GLA_TAU)
    ri = lax.broadcasted_iota(jnp.int32, (c, c), 0)
    ci = lax.broadcasted_iota(jnp.int32, (c, c), 1)
    tri = (ci <= ri).astype(BF16)
    la_hi, la_lo = _split(log_alpha)
    cum_ref[...] = _dot(tri, la_hi) + _dot(tri, la_lo)

    row_k = lax.broadcasted_iota(jnp.int32, (c, GLA_DK), 0)
    row_d = lax.broadcasted_iota(jnp.int32, (nd, GLA_DK), 0)
    lane_d = lax.broadcasted_iota(jnp.int32, (nd, c), 1)
    upper = [(row_k % (2 * b)) >= b for b in GLA_LEVELS]
    same_group = [(ri // (2 * b)) == (ci // (2 * b)) for b in GLA_LEVELS]

    for h in range(GLA_HEADS):
        ks = slice(h * GLA_DK, (h + 1) * GLA_DK)
        vs = slice(h * GLA_DV, (h + 1) * GLA_DV)
        ch = cum_ref[:, ks]
        kh = k_ref[:, ks]
        qs = q_ref[:, ks] * (GLA_DK ** -0.5)
        vt = vt_ref[vs, :]
        st = st_ref[h]

        qe = (qs * jnp.exp(ch)).astype(BF16)
        o = _dot_nt(qe, st.astype(BF16))

        p = jnp.zeros((c, c), F32)
        for lvl, b in enumerate(GLA_LEVELS):
            ref = jnp.concatenate(
                [jnp.broadcast_to(cum_ref[g + b:g + b + 1, ks], (2 * b, GLA_DK))
                 for g in range(0, c, 2 * b)], axis=0)
            e = jnp.exp(-jnp.abs(ch - ref))
            q_sc = jnp.where(upper[lvl], qs * e, 0.0).astype(BF16)
            k_sc = jnp.where(upper[lvl], 0.0, kh * e).astype(BF16)
            p = p + jnp.where(same_group[lvl], _dot_nt(q_sc, k_sc), 0.0)

        p_rows = []
        for r0 in range(0, c, nd):
            c_i = ch[r0:r0 + nd]
            q_i = qs[r0:r0 + nd]
            p_i = p[r0:r0 + nd]
            for j in range(nd):
                c_j = cum_ref[r0 + j:r0 + j + 1, ks]
                k_j = k_ref[r0 + j:r0 + j + 1, ks]
                dec = jnp.exp(jnp.where(row_d >= j, c_i - c_j, -jnp.inf))
                col = jnp.sum(q_i * k_j * dec, axis=1, keepdims=True)
                p_i = jnp.where(lane_d == r0 + j, col, p_i)
            p_rows.append(p_i)
        p = jnp.concatenate(p_rows, axis=0)
        o = o + _dot_nt(p.astype(BF16), vt)

        last = cum_ref[c - 1:c, ks]
        kd = (kh * jnp.exp(last - ch)).astype(BF16)
        st_ref[h] = st * jnp.exp(last) + _dot(vt, kd)

        on = _rms(o, gh_ref[...])
        rr = r_ref[:, vs].astype(F32)
        y_ref[:, vs] = (on * (rr * jax.nn.sigmoid(rr))).astype(y_ref.dtype)


def _gla(q, k, vt, r, a, wg, bg, gh):
    t = q.shape[0]
    c = GLA_CHUNK
    wg_hi, wg_lo = _split(wg)
    const = lambda shape: pl.BlockSpec(shape, lambda i: (0, 0))
    return pl.pallas_call(
        _gla_kernel,
        grid=(t // c,),
        in_specs=[pl.BlockSpec((c, GLA_QK), lambda i: (i, 0)),
                  pl.BlockSpec((c, GLA_QK), lambda i: (i, 0)),
                  pl.BlockSpec((GLA_V, c), lambda i: (0, i)),
                  pl.BlockSpec((c, GLA_V), lambda i: (i, 0)),
                  pl.BlockSpec((c, LANES), lambda i: (i, 0)),
                  const((LANES, GLA_QK)),
                  const((LANES, GLA_QK)),
                  const((1, GLA_QK)),
                  const((1, GLA_DV))],
        out_specs=pl.BlockSpec((c, GLA_V), lambda i: (i, 0)),
        out_shape=jax.ShapeDtypeStruct((t, GLA_V), BF16),
        scratch_shapes=[pltpu.VMEM((GLA_HEADS, GLA_DV, GLA_DK), F32),
                        pltpu.VMEM((c, GLA_QK), F32)],
        compiler_params=_cparams(("arbitrary",)),
        name="gla_scan",
    )(q, k, vt, r, a, wg_hi, wg_lo, bg.reshape(1, GLA_QK), gh.reshape(1, GLA_DV))


def _sb_sum_matrix():
    tk = SB_TK
    n = SB_FUSED * tk
    j = jnp.arange(n)[:, None]
    s = jnp.arange(n)[None, :]
    later = (j > s)
    cols = [later[:, :tk], jnp.ones((n, tk), bool)] + [later[:, i * tk:(i + 1) * tk] for i in range(1, SB_FUSED)]
    return -jnp.concatenate(cols, axis=1).astype(BF16)


def _sb_kernel(q_ref, k_ref, v_ref, u_ref, o_ref, acc_ref, carry_ref):
    tq, tk = SB_TQ, SB_TK
    rows = SB_GROUP * tq
    nf = SB_FUSED
    qi = pl.program_id(1)
    qs = jnp.concatenate([q_ref[:, g * SB_DH:(g + 1) * SB_DH] for g in range(SB_GROUP)], axis=0)
    qpos = lax.broadcasted_iota(jnp.int32, (rows, tk), 0) % tq
    kpos = lax.broadcasted_iota(jnp.int32, (rows, tk), 1)
    causal = kpos < qpos

    def tile(kt, masked):
        k0 = pl.multiple_of(kt * tk, tk)
        z = _dot_nt(qs, k_ref[pl.ds(k0, tk), :])
        lse = _log1p_exp_neg_abs(z)
        sp = jnp.maximum(z, 0.0) + lse
        spm = jnp.where(causal, sp, 0.0) if masked else sp
        spb = spm.astype(BF16)
        tot = _dot(spb, u_ref[(nf - 1) * tk:, tk:2 * tk])
        tail = _dot(spb, u_ref[(nf - 1) * tk:, nf * tk:]) + carry_ref[...]
        w = jnp.exp((z - sp) + tail)
        if masked:
            w = jnp.where(causal, w, 0.0)
        acc_ref[...] += _dot(w.astype(BF16), v_ref[pl.ds(k0, tk), :])
        carry = carry_ref[...] + tot
        carry_ref[...] = carry
        return jnp.max(carry)

    def first_fused():
        k0 = pl.multiple_of((qi - (nf - 1)) * tk, tk)
        z = _dot_nt(qs, k_ref[pl.ds(k0, nf * tk), :])
        lse = _log1p_exp_neg_abs(z)
        sp = jnp.maximum(z, 0.0) + lse
        spm = jnp.concatenate([sp[:, :(nf - 1) * tk],
                               jnp.where(causal, sp[:, (nf - 1) * tk:], 0.0)], axis=1)
        spb = spm.astype(BF16)
        first = _dot(spb, u_ref[:, :2 * tk])
        tails = [first[:, :tk]]
        for i in range(1, nf):
            tails.append(_dot(spb[:, i * tk:], u_ref[i * tk:, (i + 1) * tk:(i + 2) * tk]))
        tot = first[:, tk:]
        w = jnp.exp((z - sp) + jnp.concatenate(tails, axis=1))
        w = jnp.concatenate([w[:, :(nf - 1) * tk],
                             jnp.where(causal, w[:, (nf - 1) * tk:], 0.0)], axis=1)
        acc_ref[...] = _dot(w.astype(BF16), v_ref[pl.ds(k0, nf * tk), :])
        carry_ref[...] = tot
        return qi - nf, jnp.max(tot)

    def first_single():
        acc_ref[...] = jnp.zeros_like(acc_ref)
        carry_ref[...] = jnp.zeros_like(carry_ref)
        return qi - 1, tile(qi, True)

    start = lax.cond(qi >= nf - 1, first_fused, first_single)

    def cond(state):
        kt, mx = state
        return jnp.logical_and(kt >= 0, mx > SB_EXIT)

    def body(state):
        kt, _ = state
        return kt - 1, tile(kt, False)

    lax.while_loop(cond, body, start)

    for g in range(SB_GROUP):
        o_ref[:, g * SB_DH:(g + 1) * SB_DH] = acc_ref[g * tq:(g + 1) * tq, :].astype(o_ref.dtype)


def _sb_attention(q, kv):
    t = q.shape[0]
    tq = SB_TQ
    gw = SB_GROUP * SB_DH
    u = _sb_sum_matrix()
    return pl.pallas_call(
        _sb_kernel,
        grid=(SB_KVH, t // tq),
        in_specs=[pl.BlockSpec((tq, gw), lambda n, i: (i, n)),
                  pl.BlockSpec((t, SB_DH), lambda n, i: (0, n)),
                  pl.BlockSpec((t, SB_DH), lambda n, i: (0, SB_KVH + n)),
                  pl.BlockSpec(u.shape, lambda n, i: (0, 0))],
        out_specs=pl.BlockSpec((tq, gw), lambda n, i: (i, n)),
        out_shape=jax.ShapeDtypeStruct((t, SB_HEADS * SB_DH), BF16),
        scratch_shapes=[pltpu.VMEM((SB_GROUP * tq, SB_DH), F32),
                        pltpu.VMEM((SB_GROUP * tq, SB_TK), F32)],
        compiler_params=_cparams(("arbitrary", "arbitrary")),
        name="stick_breaking",
    )(q, kv, kv, u)


def _row_tile(t, want):
    return want if t % want == 0 else t


def kernel(x, norm1_g, norm2_g, gla_w_in, gla_w_gate_up, gla_b_gate, gla_head_g, gla_w_out,
           kv_norm_g, kv_w, sb_w_q, sb_w_out, mlp_w_up, mlp_w_down, final_g):
    b, t, d = x.shape
    assert b == 1 and d == D_MODEL and t % GLA_CHUNK == 0
    x2 = x.reshape(t, d)
    tm_big = _row_tile(t, 1024)
    tn = 1024
    tm = _row_tile(t, 512)
    tf = 1024

    w_in = gla_w_in[0]
    o1 = GLA_QK
    o2 = 2 * GLA_QK
    o3 = o2 + GLA_V
    o4 = o3 + GLA_V
    w_q = w_in[:, :o1].astype(BF16)
    w_k = w_in[:, o1:o2].astype(BF16)
    w_vt = w_in[:, o2:o3].T.astype(BF16)
    w_r = w_in[:, o3:o4].astype(BF16)
    w_a = jnp.pad(w_in[:, o4:], ((0, 0), (0, LANES - GLA_RANK))).astype(BF16)
    w_g = jnp.pad(gla_w_gate_up[0], ((0, LANES - GLA_RANK), (0, 0)))

    xn = _norm(x2, norm1_g[0], tm)
    q = _mm(xn, w_q, F32, tm=tm_big, tn=tn, name="gla_q")
    k = _mm(xn, w_k, F32, tm=tm_big, tn=tn, name="gla_k")
    vt = _mm_t(xn, w_vt, BF16, tm=tm_big, tn=tn, name="gla_vt")
    r = _mm(xn, w_r, BF16, tm=tm_big, tn=tn, name="gla_r")
    a = _mm(xn, w_a, F32, tm=tm_big, tn=LANES, name="gla_a")
    y = _gla(q, k, vt, r, a, w_g, gla_b_gate[0], gla_head_g[0])
    h, xn = _mm_res_norm(y, gla_w_out[0].astype(BF16), x2, norm2_g[0], tm=tm, name="gla_out")
    h, xn_q, xn_kv = _mlp(xn, h, mlp_w_up[0].astype(BF16), mlp_w_down[0].astype(BF16),
                          norm1_g[1], kv_norm_g, tm=tm, tf=tf, final=False, name="mlp0")

    q2 = _mm(xn_q, sb_w_q[0].astype(BF16), BF16, tm=tm_big, tn=tn,
             scale=SB_DH ** -0.5, name="sb_q")
    kv = _mm(xn_kv, kv_w.astype(BF16), BF16, tm=tm_big, tn=tn, name="shared_kv")
    o = _sb_attention(q2, kv)
    h, xn = _mm_res_norm(o, sb_w_out[0].astype(BF16), h, norm2_g[1], tm=tm, name="sb_out")
    out = _mlp(xn, h, mlp_w_up[1].astype(BF16), mlp_w_down[1].astype(BF16),
               final_g, final_g, tm=tm, tf=tf, final=True, name="mlp1")
    return out.reshape(b, t, d)
```

```python
import functools

import jax
import jax.numpy as jnp
from jax import lax
from jax.experimental import pallas as pl
from jax.experimental.pallas import tpu as pltpu

F32 = jnp.float32
BF16 = jnp.bfloat16

EPS = 1e-6
D_MODEL = 2048
D_FF = 4 * D_MODEL

GLA_HEADS = 4
GLA_DK = 256
GLA_DV = 512
GLA_QK = GLA_HEADS * GLA_DK
GLA_V = GLA_HEADS * GLA_DV
GLA_RANK = 16
GLA_TAU = 16.0
GLA_CHUNK = 128
GLA_DIAG = 8
GLA_LEVELS = (64, 32, 16, 8)
GLA_SAFE_SPAN = 60.0

SB_HEADS = 16
SB_DH = 128
SB_KVH = 4
SB_GROUP = SB_HEADS // SB_KVH
SB_KV = SB_KVH * SB_DH
SB_TQ = 128
SB_TK = 128
SB_FUSED = 3
SB_QB = 2
SB_EXIT = -110.0

LANES = 128
VMEM_LIMIT = 56 * 1024 * 1024

NT_DIMS = (((1,), (1,)), ((), ()))


def _cparams(sem):
    return pltpu.CompilerParams(dimension_semantics=sem, vmem_limit_bytes=VMEM_LIMIT)


def _rms(x, g):
    return x * lax.rsqrt(jnp.mean(x * x, axis=-1, keepdims=True) + EPS) * g


def _split(x):
    hi = x.astype(BF16)
    lo = (x - hi.astype(F32)).astype(BF16)
    return hi, lo


def _dot(a, b):
    return jnp.dot(a, b, preferred_element_type=F32)


def _dot_nt(a, b):
    return lax.dot_general(a, b, NT_DIMS, preferred_element_type=F32)


def _log1p_exp_neg_abs(z):
    return jnp.log(1.0 + jnp.exp(-jnp.abs(z)))


def _norm_kernel(x_ref, g_ref, o_ref):
    o_ref[...] = _rms(x_ref[...], g_ref[...]).astype(o_ref.dtype)


def _norm(x, g, tm):
    t, d = x.shape
    return pl.pallas_call(
        _norm_kernel,
        grid=(t // tm,),
        in_specs=[pl.BlockSpec((tm, d), lambda i: (i, 0)),
                  pl.BlockSpec((1, d), lambda i: (0, 0))],
        out_specs=pl.BlockSpec((tm, d), lambda i: (i, 0)),
        out_shape=jax.ShapeDtypeStruct((t, d), BF16),
        compiler_params=_cparams(("parallel",)),
        name="rmsnorm",
    )(x, g.reshape(1, d))


def _mm_kernel(x_ref, w_ref, o_ref, *, nt, scale):
    if nt:
        acc = _dot_nt(w_ref[...], x_ref[...])
    else:
        acc = _dot(x_ref[...], w_ref[...])
    if scale != 1.0:
        acc = acc * scale
    o_ref[...] = acc.astype(o_ref.dtype)


def _mm(x, w, out_dtype, *, tm, tn, scale=1.0, cols=None, name="proj"):
    t, k = x.shape
    c0, n = cols if cols is not None else (0, w.shape[1])
    tn = min(tn, n)
    assert c0 % tn == 0 and n % tn == 0
    j0 = c0 // tn
    return pl.pallas_call(
        functools.partial(_mm_kernel, nt=False, scale=scale),
        grid=(t // tm, n // tn),
        in_specs=[pl.BlockSpec((tm, k), lambda i, j: (i, 0)),
                  pl.BlockSpec((k, tn), lambda i, j: (0, j0 + j))],
        out_specs=pl.BlockSpec((tm, tn), lambda i, j: (i, j)),
        out_shape=jax.ShapeDtypeStruct((t, n), out_dtype),
        compiler_params=_cparams(("parallel", "arbitrary")),
        name=name,
    )(x, w)


def _mm_t(x, wt, out_dtype, *, tm, tn, name="proj_t"):
    t, k = x.shape
    n = wt.shape[0]
    tn = min(tn, n)
    return pl.pallas_call(
        functools.partial(_mm_kernel, nt=True, scale=1.0),
        grid=(t // tm, n // tn),
        in_specs=[pl.BlockSpec((tm, k), lambda i, j: (i, 0)),
                  pl.BlockSpec((tn, k), lambda i, j: (j, 0))],
        out_specs=pl.BlockSpec((tn, tm), lambda i, j: (j, i)),
        out_shape=jax.ShapeDtypeStruct((n, t), out_dtype),
        compiler_params=_cparams(("parallel", "arbitrary")),
        name=name,
    )(x, wt)


def _mm_res_norm_kernel(y_ref, w_ref, res_ref, g_ref, h_ref, xn_ref):
    h = res_ref[...] + _dot(y_ref[...], w_ref[...])
    h_ref[...] = h
    xn_ref[...] = _rms(h, g_ref[...]).astype(xn_ref.dtype)


def _mm_res_norm(y, w, res, g, *, tm, name):
    t, k = y.shape
    n = w.shape[1]
    return pl.pallas_call(
        _mm_res_norm_kernel,
        grid=(t // tm,),
        in_specs=[pl.BlockSpec((tm, k), lambda i: (i, 0)),
                  pl.BlockSpec((k, n), lambda i: (0, 0)),
                  pl.BlockSpec((tm, n), lambda i: (i, 0)),
                  pl.BlockSpec((1, n), lambda i: (0, 0))],
        out_specs=[pl.BlockSpec((tm, n), lambda i: (i, 0)),
                   pl.BlockSpec((tm, n), lambda i: (i, 0))],
        out_shape=[jax.ShapeDtypeStruct((t, n), F32),
                   jax.ShapeDtypeStruct((t, n), BF16)],
        compiler_params=_cparams(("parallel",)),
        name=name,
    )(y, w, res, g.reshape(1, n))


def _mlp_kernel(xn_ref, h_ref, wu_ref, wd_ref, g1_ref, g2_ref, acc_ref, *norm_refs, final):
    f = pl.program_id(1)

    @pl.when(f == 0)
    def _():
        acc_ref[...] = h_ref[...]

    u = _dot(xn_ref[...], wu_ref[...])
    a = jnp.square(jnp.maximum(u, 0.0)).astype(BF16)
    acc_ref[...] += _dot(a, wd_ref[...])

    @pl.when(f == pl.num_programs(1) - 1)
    def _():
        h = acc_ref[...]
        hn = h * lax.rsqrt(jnp.mean(h * h, axis=-1, keepdims=True) + EPS)
        if final:
            acc_ref[...] = hn * g1_ref[...]
        else:
            xa_ref, xb_ref = norm_refs
            xa_ref[...] = (hn * g1_ref[...]).astype(BF16)
            xb_ref[...] = (hn * g2_ref[...]).astype(BF16)


def _mlp(xn, h, wu, wd, g1, g2, *, tm, tf, final, name):
    t, d = xn.shape
    ff = wu.shape[1]
    row = pl.BlockSpec((tm, d), lambda i, f: (i, 0))
    vec = pl.BlockSpec((1, d), lambda i, f: (0, 0))
    if final:
        out_specs = row
        out_shape = jax.ShapeDtypeStruct((t, d), F32)
    else:
        out_specs = [row, row, row]
        out_shape = [jax.ShapeDtypeStruct((t, d), F32),
                     jax.ShapeDtypeStruct((t, d), BF16),
                     jax.ShapeDtypeStruct((t, d), BF16)]
    return pl.pallas_call(
        functools.partial(_mlp_kernel, final=final),
        grid=(t // tm, ff // tf),
        in_specs=[row, row,
                  pl.BlockSpec((d, tf), lambda i, f: (0, f)),
                  pl.BlockSpec((tf, d), lambda i, f: (f, 0)),
                  vec, vec],
        out_specs=out_specs,
        out_shape=out_shape,
        compiler_params=_cparams(("parallel", "arbitrary")),
        name=name,
    )(xn, h, wu, wd, g1.reshape(1, d), g2.reshape(1, d))


def _gla_kernel(q_ref, k_ref, vt_ref, r_ref, a_ref, wg_ref, bg_ref, gh_ref,
                y_ref, st_ref, cum_ref):
    c = GLA_CHUNK
    nd = GLA_DIAG

    @pl.when(pl.program_id(0) == 0)
    def _():
        st_ref[...] = jnp.zeros_like(st_ref)

    z = _dot(a_ref[...], wg_ref[...]) + bg_ref[...]
    log_alpha = (jnp.minimum(z, 0.0) - _log1p_exp_neg_abs(z)) * (1.0 / GLA_TAU)
    ri = lax.broadcasted_iota(jnp.int32, (c, c), 0)
    ci = lax.broadcasted_iota(jnp.int32, (c, c), 1)
    tri = (ci <= ri).astype(BF16)
    la_hi, la_lo = _split(log_alpha)
    cum_ref[...] = _dot(tri, la_hi) + _dot(tri, la_lo)

    row_k = lax.broadcasted_iota(jnp.int32, (c, GLA_DK), 0)
    row_d = lax.broadcasted_iota(jnp.int32, (nd, GLA_DK), 0)
    lane_d = lax.broadcasted_iota(jnp.int32, (nd, c), 1)

    def scores_one_product(ks, ch, kh, qs, qe):
        k_inv = (kh * jnp.exp(-ch)).astype(BF16)
        return jnp.where(ci <= ri, _dot_nt(qe, k_inv), 0.0)

    def scores_any_decay(ks, ch, kh, qs, qe):
        p = jnp.zeros((c, c), F32)
        for b in GLA_LEVELS:
            upper = (row_k % (2 * b)) >= b
            same_group = (ri // (2 * b)) == (ci // (2 * b))
            ref = jnp.concatenate(
                [jnp.broadcast_to(cum_ref[g + b:g + b + 1, ks], (2 * b, GLA_DK))
                 for g in range(0, c, 2 * b)], axis=0)
            e = jnp.exp(-jnp.abs(ch - ref))
            q_sc = jnp.where(upper, qs * e, 0.0).astype(BF16)
            k_sc = jnp.where(upper, 0.0, kh * e).astype(BF16)
            p = p + jnp.where(same_group, _dot_nt(q_sc, k_sc), 0.0)
        p_rows = []
        for r0 in range(0, c, nd):
            c_i = ch[r0:r0 + nd]
            q_i = qs[r0:r0 + nd]
            p_i = p[r0:r0 + nd]
            for j in range(nd):
                c_j = cum_ref[r0 + j:r0 + j + 1, ks]
                k_j = k_ref[r0 + j:r0 + j + 1, ks]
                dec = jnp.exp(jnp.where(row_d >= j, c_i - c_j, -jnp.inf))
                col = jnp.sum(q_i * k_j * dec, axis=1, keepdims=True)
                p_i = jnp.where(lane_d == r0 + j, col, p_i)
            p_rows.append(p_i)
        return jnp.concatenate(p_rows, axis=0)

    def all_heads(scores):
        for h in range(GLA_HEADS):
            ks = slice(h * GLA_DK, (h + 1) * GLA_DK)
            vs = slice(h * GLA_DV, (h + 1) * GLA_DV)
            ch = cum_ref[:, ks]
            kh = k_ref[:, ks]
            qs = q_ref[:, ks] * (GLA_DK ** -0.5)
            vt = vt_ref[vs, :]
            st = st_ref[h]

            qe = (qs * jnp.exp(ch)).astype(BF16)
            o = _dot_nt(qe, st.astype(BF16))
            p = scores(ks, ch, kh, qs, qe)
            o = o + _dot_nt(p.astype(BF16), vt)

            last = cum_ref[c - 1:c, ks]
            kd = (kh * jnp.exp(last - ch)).astype(BF16)
            st_ref[h] = st * jnp.exp(last) + _dot(vt, kd)

            on = _rms(o, gh_ref[...])
            rr = r_ref[:, vs].astype(F32)
            y_ref[:, vs] = (on * (rr * jax.nn.sigmoid(rr))).astype(y_ref.dtype)

    mild = jnp.max(-cum_ref[c - 1:c, :]) <= GLA_SAFE_SPAN

    @pl.when(mild)
    def _():
        all_heads(scores_one_product)

    @pl.when(jnp.logical_not(mild))
    def _():
        all_heads(scores_any_decay)


def _gla(q, k, vt, r, a, wg, bg, gh):
    t = q.shape[0]
    c = GLA_CHUNK
    const = lambda shape: pl.BlockSpec(shape, lambda i: (0, 0))
    return pl.pallas_call(
        _gla_kernel,
        grid=(t // c,),
        in_specs=[pl.BlockSpec((c, GLA_QK), lambda i: (i, 0)),
                  pl.BlockSpec((c, GLA_QK), lambda i: (i, 0)),
                  pl.BlockSpec((GLA_V, c), lambda i: (0, i)),
                  pl.BlockSpec((c, GLA_V), lambda i: (i, 0)),
                  pl.BlockSpec((c, LANES), lambda i: (i, 0)),
                  const((LANES, GLA_QK)),
                  const((1, GLA_QK)),
                  const((1, GLA_DV))],
        out_specs=pl.BlockSpec((c, GLA_V), lambda i: (i, 0)),
        out_shape=jax.ShapeDtypeStruct((t, GLA_V), BF16),
        scratch_shapes=[pltpu.VMEM((GLA_HEADS, GLA_DV, GLA_DK), F32),
                        pltpu.VMEM((c, GLA_QK), F32)],
        compiler_params=_cparams(("arbitrary",)),
        name="gla_scan",
    )(q, k, vt, r, a, wg.astype(BF16), bg.reshape(1, GLA_QK), gh.reshape(1, GLA_DV))


def _sb_sum_matrix():
    tk = SB_TK
    n = SB_FUSED * tk
    j = jnp.arange(n)[:, None]
    s = jnp.arange(n)[None, :]
    later = (j > s)
    cols = [later[:, :tk], jnp.ones((n, tk), bool)] + [later[:, i * tk:(i + 1) * tk] for i in range(1, SB_FUSED)]
    return -jnp.concatenate(cols, axis=1).astype(BF16)


def _sb_kernel(q_ref, k_ref, v_ref, u_ref, o_ref, acc_ref, carry_ref):
    tq, tk = SB_TQ, SB_TK
    rows = SB_GROUP * tq
    nf = SB_FUSED
    step = pl.program_id(1)
    qpos = lax.broadcasted_iota(jnp.int32, (rows, tk), 0) % tq
    kpos = lax.broadcasted_iota(jnp.int32, (rows, tk), 1)
    causal = kpos < qpos

    blocks = []
    for s in range(SB_QB):
        qs = jnp.concatenate([q_ref[s * tq:(s + 1) * tq, g * SB_DH:(g + 1) * SB_DH]
                              for g in range(SB_GROUP)], axis=0)
        blocks.append((s, step * SB_QB + s, qs))

    def tile(s, qs, kt, masked):
        k0 = pl.multiple_of(kt * tk, tk)
        z = _dot_nt(qs, k_ref[pl.ds(k0, tk), :])
        sp = jnp.maximum(z, 0.0) + _log1p_exp_neg_abs(z)
        spm = jnp.where(causal, sp, 0.0) if masked else sp
        spb = spm.astype(BF16)
        tot = _dot(spb, u_ref[(nf - 1) * tk:, tk:2 * tk])
        tail = _dot(spb, u_ref[(nf - 1) * tk:, nf * tk:]) + carry_ref[s]
        w = jnp.exp((z - sp) + tail)
        if masked:
            w = jnp.where(causal, w, 0.0)
        acc_ref[s] += _dot(w.astype(BF16), v_ref[pl.ds(k0, tk), :])
        carry = carry_ref[s] + tot
        carry_ref[s] = carry
        return jnp.max(carry)

    def first_fused(s, qi, qs):
        k0 = pl.multiple_of((qi - (nf - 1)) * tk, tk)
        z = _dot_nt(qs, k_ref[pl.ds(k0, nf * tk), :])
        sp = jnp.maximum(z, 0.0) + _log1p_exp_neg_abs(z)
        spm = jnp.concatenate([sp[:, :(nf - 1) * tk],
                               jnp.where(causal, sp[:, (nf - 1) * tk:], 0.0)], axis=1)
        spb = spm.astype(BF16)
        first = _dot(spb, u_ref[:, :2 * tk])
        tails = [first[:, :tk]]
        for i in range(1, nf):
            tails.append(_dot(spb[:, i * tk:], u_ref[i * tk:, (i + 1) * tk:(i + 2) * tk]))
        tot = first[:, tk:]
        w = jnp.exp((z - sp) + jnp.concatenate(tails, axis=1))
        w = jnp.concatenate([w[:, :(nf - 1) * tk],
                             jnp.where(causal, w[:, (nf - 1) * tk:], 0.0)], axis=1)
        acc_ref[s] = _dot(w.astype(BF16), v_ref[pl.ds(k0, nf * tk), :])
        carry_ref[s] = tot
        return qi - nf, jnp.max(tot)

    def first_single(s, qi, qs):
        acc_ref[s] = jnp.zeros((rows, SB_DH), F32)
        carry_ref[s] = jnp.zeros((rows, tk), F32)
        return qi - 1, tile(s, qs, qi, True)

    def all_fused():
        return tuple(first_fused(*blk) for blk in blocks)

    def all_single():
        return tuple(first_single(*blk) for blk in blocks)

    starts = lax.cond(step * SB_QB >= nf - 1, all_fused, all_single)

    def unfinished(state):
        kt, mx = state
        return jnp.logical_and(kt >= 0, mx > SB_EXIT)

    for (s, _, qs), start in zip(blocks, starts):
        def earlier_tile(state, s=s, qs=qs):
            kt, _ = state
            return kt - 1, tile(s, qs, kt, False)

        lax.while_loop(unfinished, earlier_tile, start)
        for g in range(SB_GROUP):
            o_ref[s * tq:(s + 1) * tq, g * SB_DH:(g + 1) * SB_DH] = (
                acc_ref[s, g * tq:(g + 1) * tq, :].astype(o_ref.dtype))


def _sb_attention(q, kv):
    t = q.shape[0]
    tq = SB_TQ * SB_QB
    gw = SB_GROUP * SB_DH
    u = _sb_sum_matrix()
    return pl.pallas_call(
        _sb_kernel,
        grid=(SB_KVH, t // tq),
        in_specs=[pl.BlockSpec((tq, gw), lambda n, i: (i, n)),
                  pl.BlockSpec((t, SB_DH), lambda n, i: (0, n)),
                  pl.BlockSpec((t, SB_DH), lambda n, i: (0, SB_KVH + n)),
                  pl.BlockSpec(u.shape, lambda n, i: (0, 0))],
        out_specs=pl.BlockSpec((tq, gw), lambda n, i: (i, n)),
        out_shape=jax.ShapeDtypeStruct((t, SB_HEADS * SB_DH), BF16),
        scratch_shapes=[pltpu.VMEM((SB_QB, SB_GROUP * SB_TQ, SB_DH), F32),
                        pltpu.VMEM((SB_QB, SB_GROUP * SB_TQ, SB_TK), F32)],
        compiler_params=_cparams(("arbitrary", "arbitrary")),
        name="stick_breaking",
    )(q, kv, kv, u)


def _row_tile(t, want):
    return want if t % want == 0 else t


def kernel(x, norm1_g, norm2_g, gla_w_in, gla_w_gate_up, gla_b_gate, gla_head_g, gla_w_out,
           kv_norm_g, kv_w, sb_w_q, sb_w_out, mlp_w_up, mlp_w_down, final_g):
    b, t, d = x.shape
    assert b == 1 and d == D_MODEL and t % GLA_CHUNK == 0
    x2 = x.reshape(t, d)
    tm_big = _row_tile(t, 1024)
    tn = 1024
    tm = _row_tile(t, 512)
    tf = 1024

    w_in = gla_w_in[0].astype(BF16)
    o_k = GLA_QK
    o_v = 2 * GLA_QK
    o_r = o_v + GLA_V
    o_a = o_r + GLA_V
    w_vt = w_in[:, o_v:o_r].T
    w_a = jnp.pad(w_in[:, o_a:], ((0, 0), (0, LANES - GLA_RANK)))
    w_g = jnp.pad(gla_w_gate_up[0], ((0, LANES - GLA_RANK), (0, 0)))

    xn = _norm(x2, norm1_g[0], tm)
    q = _mm(xn, w_in, F32, tm=tm_big, tn=tn, cols=(0, GLA_QK), name="gla_q")
    k = _mm(xn, w_in, F32, tm=tm_big, tn=tn, cols=(o_k, GLA_QK), name="gla_k")
    vt = _mm_t(xn, w_vt, BF16, tm=tm_big, tn=tn, name="gla_vt")
    r = _mm(xn, w_in, BF16, tm=tm_big, tn=tn, cols=(o_r, GLA_V), name="gla_r")
    a = _mm(xn, w_a, BF16, tm=tm_big, tn=LANES, name="gla_a")
    y = _gla(q, k, vt, r, a, w_g, gla_b_gate[0], gla_head_g[0])
    h, xn = _mm_res_norm(y, gla_w_out[0].astype(BF16), x2, norm2_g[0], tm=tm, name="gla_out")
    h, xn_q, xn_kv = _mlp(xn, h, mlp_w_up[0].astype(BF16), mlp_w_down[0].astype(BF16),
                          norm1_g[1], kv_norm_g, tm=tm, tf=tf, final=False, name="mlp0")

    q2 = _mm(xn_q, sb_w_q[0].astype(BF16), BF16, tm=tm_big, tn=tn,
             scale=SB_DH ** -0.5, name="sb_q")
    kv = _mm(xn_kv, kv_w.astype(BF16), BF16, tm=tm_big, tn=tn, name="shared_kv")
    o = _sb_attention(q2, kv)
    h, xn = _mm_res_norm(o, sb_w_out[0].astype(BF16), h, norm2_g[1], tm=tm, name="sb_out")
    out = _mlp(xn, h, mlp_w_up[1].astype(BF16), mlp_w_down[1].astype(BF16),
               final_g, final_g, tm=tm, tf=tf, final=True, name="mlp1")
    return out.reshape(b, t, d)
```

```python
import functools

import jax
import jax.numpy as jnp
from jax import lax
from jax.experimental import pallas as pl
from jax.experimental.pallas import tpu as pltpu

F32 = jnp.float32
BF16 = jnp.bfloat16

EPS = 1e-6
D_MODEL = 2048
D_FF = 4 * D_MODEL

GLA_HEADS = 4
GLA_DK = 256
GLA_DV = 512
GLA_QK = GLA_HEADS * GLA_DK
GLA_V = GLA_HEADS * GLA_DV
GLA_RANK = 16
GLA_TAU = 16.0
GLA_CHUNK = 128
GLA_DIAG = 8
GLA_LEVELS = (64, 32, 16, 8)
GLA_SAFE_SPAN = 60.0

SB_HEADS = 16
SB_DH = 128
SB_KVH = 4
SB_GROUP = SB_HEADS // SB_KVH
SB_KV = SB_KVH * SB_DH
SB_TQ = 128
SB_TK = 128
SB_FUSED = 3
SB_QB = 2
SB_EXIT = -110.0

LANES = 128
BF16_SUBLANES = 16
VMEM_LIMIT = 56 * 1024 * 1024

NT_DIMS = (((1,), (1,)), ((), ()))


def _cparams(sem):
    return pltpu.CompilerParams(dimension_semantics=sem, vmem_limit_bytes=VMEM_LIMIT)


def _rms(x, g):
    return x * lax.rsqrt(jnp.mean(x * x, axis=-1, keepdims=True) + EPS) * g


def _split(x):
    hi = x.astype(BF16)
    lo = (x - hi.astype(F32)).astype(BF16)
    return hi, lo


def _dot(a, b):
    return jnp.dot(a, b, preferred_element_type=F32)


def _dot_nt(a, b):
    return lax.dot_general(a, b, NT_DIMS, preferred_element_type=F32)


def _log1p_exp_neg_abs(z):
    return jnp.log(1.0 + jnp.exp(-jnp.abs(z)))


def _norm_kernel(x_ref, g_ref, o_ref):
    o_ref[...] = _rms(x_ref[...], g_ref[...]).astype(o_ref.dtype)


def _norm(x, g, tm):
    t, d = x.shape
    return pl.pallas_call(
        _norm_kernel,
        grid=(t // tm,),
        in_specs=[pl.BlockSpec((tm, d), lambda i: (i, 0)),
                  pl.BlockSpec((1, d), lambda i: (0, 0))],
        out_specs=pl.BlockSpec((tm, d), lambda i: (i, 0)),
        out_shape=jax.ShapeDtypeStruct((t, d), BF16),
        compiler_params=_cparams(("parallel",)),
        name="rmsnorm",
    )(x, g.reshape(1, d))


def _cast_along(weights, grid):
    steps = 1
    for g in grid:
        steps *= g

    def step_index(*ids):
        flat = ids[0]
        for g, i in zip(grid[1:], ids[1:]):
            flat = flat * g + i
        return flat

    operands, in_specs, out_specs, out_shapes = [], [], [], []
    for arr, layer in weights:
        _, rows, cols = arr.shape
        slab = rows // steps
        assert slab * steps == rows and slab % BF16_SUBLANES == 0
        operands.append(arr)
        in_specs.append(pl.BlockSpec((None, slab, cols),
                                     lambda *ids, layer=layer: (layer, step_index(*ids), 0)))
        out_specs.append(pl.BlockSpec((slab, cols), lambda *ids: (step_index(*ids), 0)))
        out_shapes.append(jax.ShapeDtypeStruct((rows, cols), BF16))
    return operands, in_specs, out_specs, out_shapes


def _store_casts(src_refs, dst_refs):
    for src, dst in zip(src_refs, dst_refs):
        dst[...] = src[...].astype(dst.dtype)


def _mm_kernel(x_ref, w_ref, *refs, nt, scale, ncast):
    o_ref = refs[ncast]
    if nt:
        acc = _dot_nt(w_ref[...], x_ref[...])
    else:
        acc = _dot(x_ref[...], w_ref[...])
    if scale != 1.0:
        acc = acc * scale
    o_ref[...] = acc.astype(o_ref.dtype)
    _store_casts(refs[:ncast], refs[ncast + 1:])


def _mm_call(x, w, out_dtype, *, tm, tn, nt, j0, n, scale, casts, name):
    t, k = x.shape
    grid = (t // tm, n // tn)
    c_ops, c_in, c_out, c_shapes = _cast_along(casts, grid)
    if nt:
        w_spec = pl.BlockSpec((tn, k), lambda i, j: (j, 0))
        o_spec = pl.BlockSpec((tn, tm), lambda i, j: (j, i))
        o_shape = jax.ShapeDtypeStruct((n, t), out_dtype)
    else:
        w_spec = pl.BlockSpec((k, tn), lambda i, j: (0, j0 + j))
        o_spec = pl.BlockSpec((tm, tn), lambda i, j: (i, j))
        o_shape = jax.ShapeDtypeStruct((t, n), out_dtype)
    outs = pl.pallas_call(
        functools.partial(_mm_kernel, nt=nt, scale=scale, ncast=len(casts)),
        grid=grid,
        in_specs=[pl.BlockSpec((tm, k), lambda i, j: (i, 0)), w_spec] + c_in,
        out_specs=[o_spec] + c_out,
        out_shape=[o_shape] + c_shapes,
        compiler_params=_cparams(("parallel", "arbitrary")),
        name=name,
    )(x, w, *c_ops)
    return outs if casts else outs[0]


def _mm(x, w, out_dtype, *, tm, tn, scale=1.0, cols=None, casts=(), name="proj"):
    c0, n = cols if cols is not None else (0, w.shape[1])
    tn = min(tn, n)
    assert c0 % tn == 0 and n % tn == 0
    return _mm_call(x, w, out_dtype, tm=tm, tn=tn, nt=False, j0=c0 // tn, n=n, scale=scale,
                    casts=casts, name=name)


def _mm_t(x, wt, out_dtype, *, tm, tn, casts=(), name="proj_t"):
    n = wt.shape[0]
    return _mm_call(x, wt, out_dtype, tm=tm, tn=min(tn, n), nt=True, j0=0, n=n, scale=1.0,
                    casts=casts, name=name)


def _mm_res_norm_kernel(y_ref, w_ref, res_ref, g_ref, *refs, ncast):
    h_ref, xn_ref = refs[ncast:ncast + 2]
    h = res_ref[...] + _dot(y_ref[...], w_ref[...])
    h_ref[...] = h
    xn_ref[...] = _rms(h, g_ref[...]).astype(xn_ref.dtype)
    _store_casts(refs[:ncast], refs[ncast + 2:])


def _mm_res_norm(y, w, res, g, *, tm, casts=(), name):
    t, k = y.shape
    n = w.shape[1]
    grid = (t // tm,)
    c_ops, c_in, c_out, c_shapes = _cast_along(casts, grid)
    return pl.pallas_call(
        functools.partial(_mm_res_norm_kernel, ncast=len(casts)),
        grid=grid,
        in_specs=[pl.BlockSpec((tm, k), lambda i: (i, 0)),
                  pl.BlockSpec((k, n), lambda i: (0, 0)),
                  pl.BlockSpec((tm, n), lambda i: (i, 0)),
                  pl.BlockSpec((1, n), lambda i: (0, 0))] + c_in,
        out_specs=[pl.BlockSpec((tm, n), lambda i: (i, 0)),
                   pl.BlockSpec((tm, n), lambda i: (i, 0))] + c_out,
        out_shape=[jax.ShapeDtypeStruct((t, n), F32),
                   jax.ShapeDtypeStruct((t, n), BF16)] + c_shapes,
        compiler_params=_cparams(("parallel",)),
        name=name,
    )(y, w, res, g.reshape(1, n), *c_ops)


def _mlp_kernel(xn_ref, h_ref, wu_ref, wd_ref, g1_ref, g2_ref, acc_ref, *norm_refs, final):
    f = pl.program_id(1)

    @pl.when(f == 0)
    def _():
        acc_ref[...] = h_ref[...]

    u = _dot(xn_ref[...], wu_ref[...])
    a = jnp.square(jnp.maximum(u, 0.0)).astype(BF16)
    acc_ref[...] += _dot(a, wd_ref[...])

    @pl.when(f == pl.num_programs(1) - 1)
    def _():
        h = acc_ref[...]
        hn = h * lax.rsqrt(jnp.mean(h * h, axis=-1, keepdims=True) + EPS)
        if final:
            acc_ref[...] = hn * g1_ref[...]
        else:
            xa_ref, xb_ref = norm_refs
            xa_ref[...] = (hn * g1_ref[...]).astype(BF16)
            xb_ref[...] = (hn * g2_ref[...]).astype(BF16)


def _mlp(xn, h, wu, wd, g1, g2, *, tm, tf, final, name):
    t, d = xn.shape
    ff = wu.shape[1]
    row = pl.BlockSpec((tm, d), lambda i, f: (i, 0))
    vec = pl.BlockSpec((1, d), lambda i, f: (0, 0))
    if final:
        out_specs = row
        out_shape = jax.ShapeDtypeStruct((t, d), F32)
    else:
        out_specs = [row, row, row]
        out_shape = [jax.ShapeDtypeStruct((t, d), F32),
                     jax.ShapeDtypeStruct((t, d), BF16),
                     jax.ShapeDtypeStruct((t, d), BF16)]
    return pl.pallas_call(
        functools.partial(_mlp_kernel, final=final),
        grid=(t // tm, ff // tf),
        in_specs=[row, row,
                  pl.BlockSpec((d, tf), lambda i, f: (0, f)),
                  pl.BlockSpec((tf, d), lambda i, f: (f, 0)),
                  vec, vec],
        out_specs=out_specs,
        out_shape=out_shape,
        compiler_params=_cparams(("parallel", "arbitrary")),
        name=name,
    )(xn, h, wu, wd, g1.reshape(1, d), g2.reshape(1, d))


def _gla_kernel(q_ref, k_ref, vt_ref, r_ref, a_ref, a_next_ref, wg_ref, bg_ref, gh_ref,
                y_ref, st_ref, cum2_ref, mild_ref):
    c = GLA_CHUNK
    nd = GLA_DIAG
    step = pl.program_id(0)
    slot = step % 2
    cum_ref = cum2_ref.at[slot]
    ri = lax.broadcasted_iota(jnp.int32, (c, c), 0)
    ci = lax.broadcasted_iota(jnp.int32, (c, c), 1)

    def gate(a, dst):
        z = _dot(a, wg_ref[...]) + bg_ref[...]
        log_alpha = (jnp.minimum(z, 0.0) - _log1p_exp_neg_abs(z)) * (1.0 / GLA_TAU)
        tri = (ci <= ri).astype(BF16)
        la_hi, la_lo = _split(log_alpha)
        cum = _dot(tri, la_hi) + _dot(tri, la_lo)
        cum2_ref[dst] = cum
        mild_ref[dst] = (jnp.max(-cum[c - 1:c, :]) <= GLA_SAFE_SPAN).astype(jnp.int32)

    @pl.when(step == 0)
    def _():
        st_ref[...] = jnp.zeros_like(st_ref)
        gate(a_ref[...], 0)

    row_k = lax.broadcasted_iota(jnp.int32, (c, GLA_DK), 0)
    row_d = lax.broadcasted_iota(jnp.int32, (nd, GLA_DK), 0)
    lane_d = lax.broadcasted_iota(jnp.int32, (nd, c), 1)

    def scores_one_product(ks, ch, kh, qs, qe):
        k_inv = (kh * jnp.exp(-ch)).astype(BF16)
        return jnp.where(ci <= ri, _dot_nt(qe, k_inv), 0.0)

    def scores_any_decay(ks, ch, kh, qs, qe):
        p = jnp.zeros((c, c), F32)
        for b in GLA_LEVELS:
            upper = (row_k % (2 * b)) >= b
            same_group = (ri // (2 * b)) == (ci // (2 * b))
            ref = jnp.concatenate(
                [jnp.broadcast_to(cum_ref[g + b:g + b + 1, ks], (2 * b, GLA_DK))
                 for g in range(0, c, 2 * b)], axis=0)
            e = jnp.exp(-jnp.abs(ch - ref))
            q_sc = jnp.where(upper, qs * e, 0.0).astype(BF16)
            k_sc = jnp.where(upper, 0.0, kh * e).astype(BF16)
            p = p + jnp.where(same_group, _dot_nt(q_sc, k_sc), 0.0)
        p_rows = []
        for r0 in range(0, c, nd):
            c_i = ch[r0:r0 + nd]
            q_i = qs[r0:r0 + nd]
            p_i = p[r0:r0 + nd]
            for j in range(nd):
                c_j = cum_ref[r0 + j:r0 + j + 1, ks]
                k_j = k_ref[r0 + j:r0 + j + 1, ks]
                dec = jnp.exp(jnp.where(row_d >= j, c_i - c_j, -jnp.inf))
                col = jnp.sum(q_i * k_j * dec, axis=1, keepdims=True)
                p_i = jnp.where(lane_d == r0 + j, col, p_i)
            p_rows.append(p_i)
        return jnp.concatenate(p_rows, axis=0)

    def all_heads(scores):
        for h in range(GLA_HEADS):
            ks = slice(h * GLA_DK, (h + 1) * GLA_DK)
            vs = slice(h * GLA_DV, (h + 1) * GLA_DV)
            ch = cum_ref[:, ks]
            kh = k_ref[:, ks]
            qs = q_ref[:, ks] * (GLA_DK ** -0.5)
            vt = vt_ref[vs, :]
            st = st_ref[h]

            qe = (qs * jnp.exp(ch)).astype(BF16)
            o = _dot_nt(qe, st.astype(BF16))
            p = scores(ks, ch, kh, qs, qe)
            o = o + _dot_nt(p.astype(BF16), vt)

            last = cum_ref[c - 1:c, ks]
            kd = (kh * jnp.exp(last - ch)).astype(BF16)
            st_ref[h] = st * jnp.exp(last) + _dot(vt, kd)

            on = _rms(o, gh_ref[...])
            rr = r_ref[:, vs].astype(F32)
            y_ref[:, vs] = (on * (rr * jax.nn.sigmoid(rr))).astype(y_ref.dtype)
        gate(a_next_ref[...], 1 - slot)

    mild = mild_ref[slot] != 0

    @pl.when(mild)
    def _():
        all_heads(scores_one_product)

    @pl.when(jnp.logical_not(mild))
    def _():
        all_heads(scores_any_decay)


def _gla(q, k, vt, r, a, wg, bg, gh):
    t = q.shape[0]
    c = GLA_CHUNK
    const = lambda shape: pl.BlockSpec(shape, lambda i: (0, 0))
    last = t // c - 1
    return pl.pallas_call(
        _gla_kernel,
        grid=(t // c,),
        in_specs=[pl.BlockSpec((c, GLA_QK), lambda i: (i, 0)),
                  pl.BlockSpec((c, GLA_QK), lambda i: (i, 0)),
                  pl.BlockSpec((GLA_V, c), lambda i: (0, i)),
                  pl.BlockSpec((c, GLA_V), lambda i: (i, 0)),
                  pl.BlockSpec((c, LANES), lambda i: (i, 0)),
                  pl.BlockSpec((c, LANES), lambda i: (jnp.minimum(i + 1, last), 0)),
                  const((LANES, GLA_QK)),
                  const((1, GLA_QK)),
                  const((1, GLA_DV))],
        out_specs=pl.BlockSpec((c, GLA_V), lambda i: (i, 0)),
        out_shape=jax.ShapeDtypeStruct((t, GLA_V), BF16),
        scratch_shapes=[pltpu.VMEM((GLA_HEADS, GLA_DV, GLA_DK), F32),
                        pltpu.VMEM((2, c, GLA_QK), F32),
                        pltpu.SMEM((2,), jnp.int32)],
        compiler_params=_cparams(("arbitrary",)),
        name="gla_scan",
    )(q, k, vt, r, a, a, wg.astype(BF16), bg.reshape(1, GLA_QK), gh.reshape(1, GLA_DV))


def _sb_sum_matrix():
    tk = SB_TK
    n = SB_FUSED * tk
    j = jnp.arange(n)[:, None]
    s = jnp.arange(n)[None, :]
    later = (j > s)
    cols = [later[:, :tk], jnp.ones((n, tk), bool)] + [later[:, i * tk:(i + 1) * tk] for i in range(1, SB_FUSED)]
    return -jnp.concatenate(cols, axis=1).astype(BF16)


def _sb_kernel(q_ref, k_ref, v_ref, u_ref, o_ref, acc_ref, carry_ref):
    tq, tk = SB_TQ, SB_TK
    rows = SB_GROUP * tq
    nf = SB_FUSED
    step = pl.program_id(1)
    qpos = lax.broadcasted_iota(jnp.int32, (rows, tk), 0) % tq
    kpos = lax.broadcasted_iota(jnp.int32, (rows, tk), 1)
    causal = kpos < qpos

    blocks = []
    for s in range(SB_QB):
        qs = jnp.concatenate([q_ref[s * tq:(s + 1) * tq, g * SB_DH:(g + 1) * SB_DH]
                              for g in range(SB_GROUP)], axis=0)
        blocks.append((s, step * SB_QB + s, qs))

    def tile(s, qs, kt, masked):
        k0 = pl.multiple_of(kt * tk, tk)
        z = _dot_nt(qs, k_ref[pl.ds(k0, tk), :])
        sp = jnp.maximum(z, 0.0) + _log1p_exp_neg_abs(z)
        spm = jnp.where(causal, sp, 0.0) if masked else sp
        spb = spm.astype(BF16)
        tot = _dot(spb, u_ref[(nf - 1) * tk:, tk:2 * tk])
        tail = _dot(spb, u_ref[(nf - 1) * tk:, nf * tk:]) + carry_ref[s]
        w = jnp.exp((z - sp) + tail)
        if masked:
            w = jnp.where(causal, w, 0.0)
        acc_ref[s] += _dot(w.astype(BF16), v_ref[pl.ds(k0, tk), :])
        carry = carry_ref[s] + tot
        carry_ref[s] = carry
        return jnp.max(carry)

    def first_fused(s, qi, qs):
        k0 = pl.multiple_of((qi - (nf - 1)) * tk, tk)
        z = _dot_nt(qs, k_ref[pl.ds(k0, nf * tk), :])
        sp = jnp.maximum(z, 0.0) + _log1p_exp_neg_abs(z)
        spm = jnp.concatenate([sp[:, :(nf - 1) * tk],
                               jnp.where(causal, sp[:, (nf - 1) * tk:], 0.0)], axis=1)
        spb = spm.astype(BF16)
        first = _dot(spb, u_ref[:, :2 * tk])
        tails = [first[:, :tk]]
        for i in range(1, nf):
            tails.append(_dot(spb[:, i * tk:], u_ref[i * tk:, (i + 1) * tk:(i + 2) * tk]))
        tot = first[:, tk:]
        w = jnp.exp((z - sp) + jnp.concatenate(tails, axis=1))
        w = jnp.concatenate([w[:, :(nf - 1) * tk],
                             jnp.where(causal, w[:, (nf - 1) * tk:], 0.0)], axis=1)
        acc_ref[s] = _dot(w.astype(BF16), v_ref[pl.ds(k0, nf * tk), :])
        carry_ref[s] = tot
        return qi - nf, jnp.max(tot)

    def first_single(s, qi, qs):
        acc_ref[s] = jnp.zeros((rows, SB_DH), F32)
        carry_ref[s] = jnp.zeros((rows, tk), F32)
        return qi - 1, tile(s, qs, qi, True)

    def all_fused():
        return tuple(first_fused(*blk) for blk in blocks)

    def all_single():
        return tuple(first_single(*blk) for blk in blocks)

    starts = lax.cond(step * SB_QB >= nf - 1, all_fused, all_single)

    def unfinished(state):
        kt, mx = state
        return jnp.logical_and(kt >= 0, mx > SB_EXIT)

    for (s, _, qs), start in zip(blocks, starts):
        def earlier_tile(state, s=s, qs=qs):
            kt, _ = state
            return kt - 1, tile(s, qs, kt, False)

        lax.while_loop(unfinished, earlier_tile, start)
        for g in range(SB_GROUP):
            o_ref[s * tq:(s + 1) * tq, g * SB_DH:(g + 1) * SB_DH] = (
                acc_ref[s, g * tq:(g + 1) * tq, :].astype(o_ref.dtype))


def _sb_attention(q, kv):
    t = q.shape[0]
    tq = SB_TQ * SB_QB
    gw = SB_GROUP * SB_DH
    u = _sb_sum_matrix()
    return pl.pallas_call(
        _sb_kernel,
        grid=(SB_KVH, t // tq),
        in_specs=[pl.BlockSpec((tq, gw), lambda n, i: (i, n)),
                  pl.BlockSpec((t, SB_DH), lambda n, i: (0, n)),
                  pl.BlockSpec((t, SB_DH), lambda n, i: (0, SB_KVH + n)),
                  pl.BlockSpec(u.shape, lambda n, i: (0, 0))],
        out_specs=pl.BlockSpec((tq, gw), lambda n, i: (i, n)),
        out_shape=jax.ShapeDtypeStruct((t, SB_HEADS * SB_DH), BF16),
        scratch_shapes=[pltpu.VMEM((SB_QB, SB_GROUP * SB_TQ, SB_DH), F32),
                        pltpu.VMEM((SB_QB, SB_GROUP * SB_TQ, SB_TK), F32)],
        compiler_params=_cparams(("arbitrary", "arbitrary")),
        name="stick_breaking",
    )(q, kv, kv, u)


def _row_tile(t, want):
    return want if t % want == 0 else t


def kernel(x, norm1_g, norm2_g, gla_w_in, gla_w_gate_up, gla_b_gate, gla_head_g, gla_w_out,
           kv_norm_g, kv_w, sb_w_q, sb_w_out, mlp_w_up, mlp_w_down, final_g):
    b, t, d = x.shape
    assert b == 1 and d == D_MODEL and t % GLA_CHUNK == 0
    x2 = x.reshape(t, d)
    tm_big = _row_tile(t, 1024)
    tn = 1024
    tm = _row_tile(t, 512)
    tf = 1024

    w_in = gla_w_in[0].astype(BF16)
    o_k = GLA_QK
    o_v = 2 * GLA_QK
    o_r = o_v + GLA_V
    o_a = o_r + GLA_V
    w_vt = w_in[:, o_v:o_r].T
    w_a = jnp.pad(w_in[:, o_a:], ((0, 0), (0, LANES - GLA_RANK)))
    w_g = jnp.pad(gla_w_gate_up[0], ((0, LANES - GLA_RANK), (0, 0)))

    kv_w3 = kv_w.reshape(1, *kv_w.shape)

    xn = _norm(x2, norm1_g[0], tm)
    q, w_out0 = _mm(xn, w_in, F32, tm=tm_big, tn=tn, cols=(0, GLA_QK),
                    casts=[(gla_w_out, 0)], name="gla_q")
    k, w_q1 = _mm(xn, w_in, F32, tm=tm_big, tn=tn, cols=(o_k, GLA_QK),
                  casts=[(sb_w_q, 0)], name="gla_k")
    vt, w_up0 = _mm_t(xn, w_vt, BF16, tm=tm_big, tn=tn, casts=[(mlp_w_up, 0)], name="gla_vt")
    r, w_down0 = _mm(xn, w_in, BF16, tm=tm_big, tn=tn, cols=(o_r, GLA_V),
                     casts=[(mlp_w_down, 0)], name="gla_r")
    a = _mm(xn, w_a, BF16, tm=tm_big, tn=LANES, name="gla_a")
    y = _gla(q, k, vt, r, a, w_g, gla_b_gate[0], gla_head_g[0])
    h, xn, w_out1, w_kv = _mm_res_norm(y, w_out0, x2, norm2_g[0], tm=tm,
                                       casts=[(sb_w_out, 0), (kv_w3, 0)], name="gla_out")
    h, xn_q, xn_kv = _mlp(xn, h, w_up0, w_down0, norm1_g[1], kv_norm_g,
                          tm=tm, tf=tf, final=False, name="mlp0")

    q2, w_up1 = _mm(xn_q, w_q1, BF16, tm=tm_big, tn=tn, scale=SB_DH ** -0.5,
                    casts=[(mlp_w_up, 1)], name="sb_q")
    kv = _mm(xn_kv, w_kv, BF16, tm=tm_big, tn=tn, name="shared_kv")
    o = _sb_attention(q2, kv)
    h, xn, w_down1 = _mm_res_norm(o, w_out1, h, norm2_g[1], tm=tm,
                                  casts=[(mlp_w_down, 1)], name="sb_out")
    out = _mlp(xn, h, w_up1, w_down1, final_g, final_g, tm=tm, tf=tf, final=True, name="mlp1")
    return out.reshape(b, t, d)
```

```python
import functools

import jax
import jax.numpy as jnp
from jax import lax
from jax.experimental import pallas as pl
from jax.experimental.pallas import tpu as pltpu

F32 = jnp.float32
BF16 = jnp.bfloat16

EPS = 1e-6
D_MODEL = 2048
D_FF = 4 * D_MODEL

GLA_HEADS = 4
GLA_DK = 256
GLA_DV = 512
GLA_QK = GLA_HEADS * GLA_DK
GLA_V = GLA_HEADS * GLA_DV
GLA_RANK = 16
GLA_TAU = 16.0
GLA_CHUNK = 128
GLA_DIAG = 8
GLA_LEVELS = (64, 32, 16, 8)
GLA_SAFE_SPAN = 60.0

SB_HEADS = 16
SB_DH = 128
SB_KVH = 4
SB_GROUP = SB_HEADS // SB_KVH
SB_KV = SB_KVH * SB_DH
SB_TQ = 128
SB_TK = 128
SB_FUSED = 3
SB_QB = 2
SB_EXIT = -110.0

LANES = 128
BF16_SUBLANES = 16
VMEM_LIMIT = 56 * 1024 * 1024

NT_DIMS = (((1,), (1,)), ((), ()))


def _cparams(sem):
    return pltpu.CompilerParams(dimension_semantics=sem, vmem_limit_bytes=VMEM_LIMIT)


def _rms(x, g):
    return x * lax.rsqrt(jnp.mean(x * x, axis=-1, keepdims=True) + EPS) * g


def _split(x):
    hi = x.astype(BF16)
    lo = (x - hi.astype(F32)).astype(BF16)
    return hi, lo


def _dot(a, b):
    return jnp.dot(a, b, preferred_element_type=F32)


def _dot_nt(a, b):
    return lax.dot_general(a, b, NT_DIMS, preferred_element_type=F32)


def _log1p_exp_neg_abs(z):
    return jnp.log(1.0 + jnp.exp(-jnp.abs(z)))


def _cast_along(weights, grid):
    steps = 1
    for g in grid:
        steps *= g

    def step_index(*ids):
        flat = ids[0]
        for g, i in zip(grid[1:], ids[1:]):
            flat = flat * g + i
        return flat

    operands, in_specs, out_specs, out_shapes = [], [], [], []
    for arr, layer in weights:
        _, rows, cols = arr.shape
        slab = rows // steps
        assert slab * steps == rows and slab % BF16_SUBLANES == 0
        operands.append(arr)
        in_specs.append(pl.BlockSpec((None, slab, cols),
                                     lambda *ids, layer=layer: (layer, step_index(*ids), 0)))
        out_specs.append(pl.BlockSpec((slab, cols), lambda *ids: (step_index(*ids), 0)))
        out_shapes.append(jax.ShapeDtypeStruct((rows, cols), BF16))
    return operands, in_specs, out_specs, out_shapes


def _store_casts(src_refs, dst_refs):
    for src, dst in zip(src_refs, dst_refs):
        dst[...] = src[...].astype(dst.dtype)


def _mm_kernel(x_ref, w_ref, *refs, nt, scale, ncast):
    o_ref = refs[ncast]
    if nt:
        acc = _dot_nt(w_ref[...], x_ref[...])
    else:
        acc = _dot(x_ref[...], w_ref[...])
    if scale != 1.0:
        acc = acc * scale
    o_ref[...] = acc.astype(o_ref.dtype)
    _store_casts(refs[:ncast], refs[ncast + 1:])


def _mm_call(x, w, out_dtype, *, tm, tn, nt, j0, n, scale, casts, name):
    t, k = x.shape
    grid = (t // tm, n // tn)
    c_ops, c_in, c_out, c_shapes = _cast_along(casts, grid)
    if nt:
        w_spec = pl.BlockSpec((tn, k), lambda i, j: (j, 0))
        o_spec = pl.BlockSpec((tn, tm), lambda i, j: (j, i))
        o_shape = jax.ShapeDtypeStruct((n, t), out_dtype)
    else:
        w_spec = pl.BlockSpec((k, tn), lambda i, j: (0, j0 + j))
        o_spec = pl.BlockSpec((tm, tn), lambda i, j: (i, j))
        o_shape = jax.ShapeDtypeStruct((t, n), out_dtype)
    outs = pl.pallas_call(
        functools.partial(_mm_kernel, nt=nt, scale=scale, ncast=len(casts)),
        grid=grid,
        in_specs=[pl.BlockSpec((tm, k), lambda i, j: (i, 0)), w_spec] + c_in,
        out_specs=[o_spec] + c_out,
        out_shape=[o_shape] + c_shapes,
        compiler_params=_cparams(("parallel", "arbitrary")),
        name=name,
    )(x, w, *c_ops)
    return outs if casts else outs[0]


def _mm(x, w, out_dtype, *, tm, tn, scale=1.0, cols=None, casts=(), name="proj"):
    c0, n = cols if cols is not None else (0, w.shape[1])
    tn = min(tn, n)
    assert c0 % tn == 0 and n % tn == 0
    return _mm_call(x, w, out_dtype, tm=tm, tn=tn, nt=False, j0=c0 // tn, n=n, scale=scale,
                    casts=casts, name=name)


def _mm_t(x, wt, out_dtype, *, tm, tn, casts=(), name="proj_t"):
    n = wt.shape[0]
    return _mm_call(x, wt, out_dtype, tm=tm, tn=min(tn, n), nt=True, j0=0, n=n, scale=1.0,
                    casts=casts, name=name)


def _norm_mm_kernel(x_ref, g_ref, w_ref, *refs, ncast):
    o_ref, xn_ref = refs[ncast:ncast + 2]
    xn = _rms(x_ref[...], g_ref[...]).astype(xn_ref.dtype)
    xn_ref[...] = xn
    o_ref[...] = _dot(xn, w_ref[...]).astype(o_ref.dtype)
    _store_casts(refs[:ncast], refs[ncast + 2:])


def _norm_mm(x, g, w, out_dtype, *, tm, cols, casts=(), name):
    t, k = x.shape
    c0, n = cols
    assert c0 % n == 0
    grid = (t // tm,)
    c_ops, c_in, c_out, c_shapes = _cast_along(casts, grid)
    return pl.pallas_call(
        functools.partial(_norm_mm_kernel, ncast=len(casts)),
        grid=grid,
        in_specs=[pl.BlockSpec((tm, k), lambda i: (i, 0)),
                  pl.BlockSpec((1, k), lambda i: (0, 0)),
                  pl.BlockSpec((k, n), lambda i: (0, c0 // n))] + c_in,
        out_specs=[pl.BlockSpec((tm, n), lambda i: (i, 0)),
                   pl.BlockSpec((tm, k), lambda i: (i, 0))] + c_out,
        out_shape=[jax.ShapeDtypeStruct((t, n), out_dtype),
                   jax.ShapeDtypeStruct((t, k), BF16)] + c_shapes,
        compiler_params=_cparams(("parallel",)),
        name=name,
    )(x, g.reshape(1, k), w, *c_ops)


def _mm_res_norm_kernel(y_ref, w_ref, res_ref, g_ref, *refs, ncast):
    h_ref, xn_ref = refs[ncast:ncast + 2]
    h = res_ref[...] + _dot(y_ref[...], w_ref[...])
    h_ref[...] = h
    xn_ref[...] = _rms(h, g_ref[...]).astype(xn_ref.dtype)
    _store_casts(refs[:ncast], refs[ncast + 2:])


def _mm_res_norm(y, w, res, g, *, tm, casts=(), name):
    t, k = y.shape
    n = w.shape[1]
    grid = (t // tm,)
    c_ops, c_in, c_out, c_shapes = _cast_along(casts, grid)
    return pl.pallas_call(
        functools.partial(_mm_res_norm_kernel, ncast=len(casts)),
        grid=grid,
        in_specs=[pl.BlockSpec((tm, k), lambda i: (i, 0)),
                  pl.BlockSpec((k, n), lambda i: (0, 0)),
                  pl.BlockSpec((tm, n), lambda i: (i, 0)),
                  pl.BlockSpec((1, n), lambda i: (0, 0))] + c_in,
        out_specs=[pl.BlockSpec((tm, n), lambda i: (i, 0)),
                   pl.BlockSpec((tm, n), lambda i: (i, 0))] + c_out,
        out_shape=[jax.ShapeDtypeStruct((t, n), F32),
                   jax.ShapeDtypeStruct((t, n), BF16)] + c_shapes,
        compiler_params=_cparams(("parallel",)),
        name=name,
    )(y, w, res, g.reshape(1, n), *c_ops)


def _mlp_kernel(xn_ref, h_ref, wu_ref, wd_ref, g1_ref, g2_ref, acc_ref, *norm_refs, final):
    f = pl.program_id(1)

    @pl.when(f == 0)
    def _():
        acc_ref[...] = h_ref[...]

    u = _dot(xn_ref[...], wu_ref[...])
    a = jnp.square(jnp.maximum(u, 0.0)).astype(BF16)
    acc_ref[...] += _dot(a, wd_ref[...])

    @pl.when(f == pl.num_programs(1) - 1)
    def _():
        h = acc_ref[...]
        hn = h * lax.rsqrt(jnp.mean(h * h, axis=-1, keepdims=True) + EPS)
        if final:
            acc_ref[...] = hn * g1_ref[...]
        else:
            xa_ref, xb_ref = norm_refs
            xa_ref[...] = (hn * g1_ref[...]).astype(BF16)
            xb_ref[...] = (hn * g2_ref[...]).astype(BF16)


def _mlp(xn, h, wu, wd, g1, g2, *, tm, tf, final, name):
    t, d = xn.shape
    ff = wu.shape[1]
    row = pl.BlockSpec((tm, d), lambda i, f: (i, 0))
    vec = pl.BlockSpec((1, d), lambda i, f: (0, 0))
    if final:
        out_specs = row
        out_shape = jax.ShapeDtypeStruct((t, d), F32)
    else:
        out_specs = [row, row, row]
        out_shape = [jax.ShapeDtypeStruct((t, d), F32),
                     jax.ShapeDtypeStruct((t, d), BF16),
                     jax.ShapeDtypeStruct((t, d), BF16)]
    return pl.pallas_call(
        functools.partial(_mlp_kernel, final=final),
        grid=(t // tm, ff // tf),
        in_specs=[row, row,
                  pl.BlockSpec((d, tf), lambda i, f: (0, f)),
                  pl.BlockSpec((tf, d), lambda i, f: (f, 0)),
                  vec, vec],
        out_specs=out_specs,
        out_shape=out_shape,
        compiler_params=_cparams(("parallel", "arbitrary")),
        name=name,
    )(xn, h, wu, wd, g1.reshape(1, d), g2.reshape(1, d))


def _gla_kernel(q_ref, k_ref, vt_ref, r_ref, a_ref, a_next_ref, wg_ref, bg_ref, gh_ref,
                y_ref, st_ref, cum2_ref, mild_ref):
    c = GLA_CHUNK
    nd = GLA_DIAG
    step = pl.program_id(0)
    slot = step % 2
    cum_ref = cum2_ref.at[slot]
    ri = lax.broadcasted_iota(jnp.int32, (c, c), 0)
    ci = lax.broadcasted_iota(jnp.int32, (c, c), 1)

    def gate(a, dst):
        z = _dot(a, wg_ref[...]) + bg_ref[...]
        log_alpha = (jnp.minimum(z, 0.0) - _log1p_exp_neg_abs(z)) * (1.0 / GLA_TAU)
        tri = (ci <= ri).astype(BF16)
        la_hi, la_lo = _split(log_alpha)
        cum = _dot(tri, la_hi) + _dot(tri, la_lo)
        cum2_ref[dst] = cum
        mild_ref[dst] = (jnp.max(-cum[c - 1:c, :]) <= GLA_SAFE_SPAN).astype(jnp.int32)

    @pl.when(step == 0)
    def _():
        st_ref[...] = jnp.zeros_like(st_ref)
        gate(a_ref[...], 0)

    row_k = lax.broadcasted_iota(jnp.int32, (c, GLA_DK), 0)
    row_d = lax.broadcasted_iota(jnp.int32, (nd, GLA_DK), 0)
    lane_d = lax.broadcasted_iota(jnp.int32, (nd, c), 1)

    def scores_one_product(ks, ch, kh, qs, qe):
        k_inv = (kh * jnp.exp(-ch)).astype(BF16)
        return jnp.where(ci <= ri, _dot_nt(qe, k_inv), 0.0)

    def scores_any_decay(ks, ch, kh, qs, qe):
        p = jnp.zeros((c, c), F32)
        for b in GLA_LEVELS:
            upper = (row_k % (2 * b)) >= b
            same_group = (ri // (2 * b)) == (ci // (2 * b))
            ref = jnp.concatenate(
                [jnp.broadcast_to(cum_ref[g + b:g + b + 1, ks], (2 * b, GLA_DK))
                 for g in range(0, c, 2 * b)], axis=0)
            e = jnp.exp(-jnp.abs(ch - ref))
            q_sc = jnp.where(upper, qs * e, 0.0).astype(BF16)
            k_sc = jnp.where(upper, 0.0, kh * e).astype(BF16)
            p = p + jnp.where(same_group, _dot_nt(q_sc, k_sc), 0.0)
        p_rows = []
        for r0 in range(0, c, nd):
            c_i = ch[r0:r0 + nd]
            q_i = qs[r0:r0 + nd]
            p_i = p[r0:r0 + nd]
            for j in range(nd):
                c_j = cum_ref[r0 + j:r0 + j + 1, ks]
                k_j = k_ref[r0 + j:r0 + j + 1, ks]
                dec = jnp.exp(jnp.where(row_d >= j, c_i - c_j, -jnp.inf))
                col = jnp.sum(q_i * k_j * dec, axis=1, keepdims=True)
                p_i = jnp.where(lane_d == r0 + j, col, p_i)
            p_rows.append(p_i)
        return jnp.concatenate(p_rows, axis=0)

    def all_heads(scores):
        for h in range(GLA_HEADS):
            ks = slice(h * GLA_DK, (h + 1) * GLA_DK)
            vs = slice(h * GLA_DV, (h + 1) * GLA_DV)
            ch = cum_ref[:, ks]
            kh = k_ref[:, ks]
            qs = q_ref[:, ks] * (GLA_DK ** -0.5)
            vt = vt_ref[vs, :]
            st = st_ref[h]

            qe = (qs * jnp.exp(ch)).astype(BF16)
            o = _dot_nt(qe, st.astype(BF16))
            p = scores(ks, ch, kh, qs, qe)
            o = o + _dot_nt(p.astype(BF16), vt)

            last = cum_ref[c - 1:c, ks]
            kd = (kh * jnp.exp(last - ch)).astype(BF16)
            st_ref[h] = st * jnp.exp(last) + _dot(vt, kd)

            on = _rms(o, gh_ref[...])
            rr = r_ref[:, vs].astype(F32)
            y_ref[:, vs] = (on * (rr * jax.nn.sigmoid(rr))).astype(y_ref.dtype)
        gate(a_next_ref[...], 1 - slot)

    mild = mild_ref[slot] != 0

    @pl.when(mild)
    def _():
        all_heads(scores_one_product)

    @pl.when(jnp.logical_not(mild))
    def _():
        all_heads(scores_any_decay)


def _gla(q, k, vt, r, a, wg, bg, gh):
    t = q.shape[0]
    c = GLA_CHUNK
    const = lambda shape: pl.BlockSpec(shape, lambda i: (0, 0))
    last = t // c - 1
    return pl.pallas_call(
        _gla_kernel,
        grid=(t // c,),
        in_specs=[pl.BlockSpec((c, GLA_QK), lambda i: (i, 0)),
                  pl.BlockSpec((c, GLA_QK), lambda i: (i, 0)),
                  pl.BlockSpec((GLA_V, c), lambda i: (0, i)),
                  pl.BlockSpec((c, GLA_V), lambda i: (i, 0)),
                  pl.BlockSpec((c, LANES), lambda i: (i, 0)),
                  pl.BlockSpec((c, LANES), lambda i: (jnp.minimum(i + 1, last), 0)),
                  const((LANES, GLA_QK)),
                  const((1, GLA_QK)),
                  const((1, GLA_DV))],
        out_specs=pl.BlockSpec((c, GLA_V), lambda i: (i, 0)),
        out_shape=jax.ShapeDtypeStruct((t, GLA_V), BF16),
        scratch_shapes=[pltpu.VMEM((GLA_HEADS, GLA_DV, GLA_DK), F32),
                        pltpu.VMEM((2, c, GLA_QK), F32),
                        pltpu.SMEM((2,), jnp.int32)],
        compiler_params=_cparams(("arbitrary",)),
        name="gla_scan",
    )(q, k, vt, r, a, a, wg.astype(BF16), bg.reshape(1, GLA_QK), gh.reshape(1, GLA_DV))


def _sb_sum_matrix():
    tk = SB_TK
    n = SB_FUSED * tk
    j = jnp.arange(n)[:, None]
    s = jnp.arange(n)[None, :]
    later = (j > s)
    cols = [later[:, :tk], jnp.ones((n, tk), bool)] + [later[:, i * tk:(i + 1) * tk] for i in range(1, SB_FUSED)]
    return -jnp.concatenate(cols, axis=1).astype(BF16)


def _sb_kernel(q_ref, kt_ref, v_ref, u_ref, o_ref, acc_ref, carry_ref):
    tq, tk = SB_TQ, SB_TK
    rows = SB_GROUP * tq
    nf = SB_FUSED
    step = pl.program_id(1)
    qpos = lax.broadcasted_iota(jnp.int32, (rows, tk), 0) % tq
    kpos = lax.broadcasted_iota(jnp.int32, (rows, tk), 1)
    causal = kpos < qpos

    blocks = []
    for s in range(SB_QB):
        qs = jnp.concatenate([q_ref[s * tq:(s + 1) * tq, g * SB_DH:(g + 1) * SB_DH]
                              for g in range(SB_GROUP)], axis=0)
        blocks.append((s, step * SB_QB + s, qs))

    def tile(s, qs, kt, masked):
        k0 = pl.multiple_of(kt * tk, tk)
        z = _dot(qs, kt_ref[:, pl.ds(k0, tk)])
        sp = jnp.maximum(z, 0.0) + _log1p_exp_neg_abs(z)
        spm = jnp.where(causal, sp, 0.0) if masked else sp
        spb = spm.astype(BF16)
        tot = _dot(spb, u_ref[(nf - 1) * tk:, tk:2 * tk])
        tail = _dot(spb, u_ref[(nf - 1) * tk:, nf * tk:]) + carry_ref[s]
        w = jnp.exp((z - sp) + tail)
        if masked:
            w = jnp.where(causal, w, 0.0)
        acc_ref[s] += _dot(w.astype(BF16), v_ref[pl.ds(k0, tk), :])
        carry = carry_ref[s] + tot
        carry_ref[s] = carry
        return jnp.max(carry)

    def first_fused(s, qi, qs):
        k0 = pl.multiple_of((qi - (nf - 1)) * tk, tk)
        z = _dot(qs, kt_ref[:, pl.ds(k0, nf * tk)])
        sp = jnp.maximum(z, 0.0) + _log1p_exp_neg_abs(z)
        spm = jnp.concatenate([sp[:, :(nf - 1) * tk],
                               jnp.where(causal, sp[:, (nf - 1) * tk:], 0.0)], axis=1)
        spb = spm.astype(BF16)
        first = _dot(spb, u_ref[:, :2 * tk])
        later = _dot(spb[:, tk:], u_ref[tk:, 2 * tk:])
        tot = first[:, tk:]
        w = jnp.exp((z - sp) + jnp.concatenate([first[:, :tk], later], axis=1))
        w = jnp.concatenate([w[:, :(nf - 1) * tk],
                             jnp.where(causal, w[:, (nf - 1) * tk:], 0.0)], axis=1)
        acc_ref[s] = _dot(w.astype(BF16), v_ref[pl.ds(k0, nf * tk), :])
        carry_ref[s] = tot
        return qi - nf, jnp.max(tot)

    def first_single(s, qi, qs):
        acc_ref[s] = jnp.zeros((rows, SB_DH), F32)
        carry_ref[s] = jnp.zeros((rows, tk), F32)
        return qi - 1, tile(s, qs, qi, True)

    def all_fused():
        return tuple(first_fused(*blk) for blk in blocks)

    def all_single():
        return tuple(first_single(*blk) for blk in blocks)

    starts = lax.cond(step * SB_QB >= nf - 1, all_fused, all_single)

    def unfinished(state):
        kt, mx = state
        return jnp.logical_and(kt >= 0, mx > SB_EXIT)

    for (s, _, qs), start in zip(blocks, starts):
        def earlier_tile(state, s=s, qs=qs):
            kt, _ = state
            return kt - 1, tile(s, qs, kt, False)

        lax.while_loop(unfinished, earlier_tile, start)
        for g in range(SB_GROUP):
            o_ref[s * tq:(s + 1) * tq, g * SB_DH:(g + 1) * SB_DH] = (
                acc_ref[s, g * tq:(g + 1) * tq, :].astype(o_ref.dtype))


def _sb_attention(q, kt, v):
    t = q.shape[0]
    tq = SB_TQ * SB_QB
    gw = SB_GROUP * SB_DH
    u = _sb_sum_matrix()
    return pl.pallas_call(
        _sb_kernel,
        grid=(SB_KVH, t // tq),
        in_specs=[pl.BlockSpec((tq, gw), lambda n, i: (i, n)),
                  pl.BlockSpec((SB_DH, t), lambda n, i: (n, 0)),
                  pl.BlockSpec((t, SB_DH), lambda n, i: (0, n)),
                  pl.BlockSpec(u.shape, lambda n, i: (0, 0))],
        out_specs=pl.BlockSpec((tq, gw), lambda n, i: (i, n)),
        out_shape=jax.ShapeDtypeStruct((t, SB_HEADS * SB_DH), BF16),
        scratch_shapes=[pltpu.VMEM((SB_QB, SB_GROUP * SB_TQ, SB_DH), F32),
                        pltpu.VMEM((SB_QB, SB_GROUP * SB_TQ, SB_TK), F32)],
        compiler_params=_cparams(("arbitrary", "arbitrary")),
        name="stick_breaking",
    )(q, kt, v, u)


def _row_tile(t, want):
    return want if t % want == 0 else t


def kernel(x, norm1_g, norm2_g, gla_w_in, gla_w_gate_up, gla_b_gate, gla_head_g, gla_w_out,
           kv_norm_g, kv_w, sb_w_q, sb_w_out, mlp_w_up, mlp_w_down, final_g):
    b, t, d = x.shape
    assert b == 1 and d == D_MODEL and t % GLA_CHUNK == 0
    x2 = x.reshape(t, d)
    tm_big = _row_tile(t, 1024)
    tn = 1024
    tm = _row_tile(t, 512)
    tf = 1024

    w_in = gla_w_in[0].astype(BF16)
    o_k = GLA_QK
    o_v = 2 * GLA_QK
    o_r = o_v + GLA_V
    o_a = o_r + GLA_V
    w_vt = w_in[:, o_v:o_r].T
    w_a = jnp.pad(w_in[:, o_a:], ((0, 0), (0, LANES - GLA_RANK)))
    w_g = jnp.pad(gla_w_gate_up[0], ((0, LANES - GLA_RANK), (0, 0)))
    w_kt = kv_w[:, :SB_KV].T.astype(BF16)

    kv_w3 = kv_w.reshape(1, *kv_w.shape)

    q, xn, w_out0 = _norm_mm(x2, norm1_g[0], w_in, F32, tm=tm_big, cols=(0, GLA_QK),
                             casts=[(gla_w_out, 0)], name="gla_q")
    k, w_q1 = _mm(xn, w_in, F32, tm=tm_big, tn=tn, cols=(o_k, GLA_QK),
                  casts=[(sb_w_q, 0)], name="gla_k")
    vt, w_up0 = _mm_t(xn, w_vt, BF16, tm=tm_big, tn=tn, casts=[(mlp_w_up, 0)], name="gla_vt")
    r, w_down0 = _mm(xn, w_in, BF16, tm=tm_big, tn=tn, cols=(o_r, GLA_V),
                     casts=[(mlp_w_down, 0)], name="gla_r")
    a = _mm(xn, w_a, BF16, tm=tm_big, tn=LANES, name="gla_a")
    y = _gla(q, k, vt, r, a, w_g, gla_b_gate[0], gla_head_g[0])
    h, xn, w_out1, w_kv = _mm_res_norm(y, w_out0, x2, norm2_g[0], tm=tm,
                                       casts=[(sb_w_out, 0), (kv_w3, 0)], name="gla_out")
    h, xn_q, xn_kv = _mlp(xn, h, w_up0, w_down0, norm1_g[1], kv_norm_g,
                          tm=tm, tf=tf, final=False, name="mlp0")

    q2, w_up1 = _mm(xn_q, w_q1, BF16, tm=tm_big, tn=tn, scale=SB_DH ** -0.5,
                    casts=[(mlp_w_up, 1)], name="sb_q")
    kt = _mm_t(xn_kv, w_kt, BF16, tm=tm_big, tn=tn, name="shared_kt")
    v = _mm(xn_kv, w_kv, BF16, tm=tm_big, tn=tn, cols=(SB_KV, SB_KV), name="shared_v")
    o = _sb_attention(q2, kt, v)
    h, xn, w_down1 = _mm_res_norm(o, w_out1, h, norm2_g[1], tm=tm,
                                  casts=[(mlp_w_down, 1)], name="sb_out")
    out = _mlp(xn, h, w_up1, w_down1, final_g, final_g, tm=tm, tf=tf, final=True, name="mlp1")
    return out.reshape(b, t, d)
```

```python
import functools

import jax
import jax.numpy as jnp
from jax import lax
from jax.experimental import pallas as pl
from jax.experimental.pallas import tpu as pltpu

F32 = jnp.float32
BF16 = jnp.bfloat16

EPS = 1e-6
D_MODEL = 2048
D_FF = 4 * D_MODEL

GLA_HEADS = 4
GLA_DK = 256
GLA_DV = 512
GLA_QK = GLA_HEADS * GLA_DK
GLA_V = GLA_HEADS * GLA_DV
GLA_RANK = 16
GLA_TAU = 16.0
GLA_CHUNK = 128
GLA_DIAG = 8
GLA_LEVELS = (64, 32, 16, 8)
GLA_SAFE_SPAN = 60.0

SB_HEADS = 16
SB_DH = 128
SB_KVH = 4
SB_GROUP = SB_HEADS // SB_KVH
SB_KV = SB_KVH * SB_DH
SB_TQ = 64
SB_TK = 64
SB_FUSED = 4
SB_QB = 4
SB_EXIT = -110.0

LANES = 128
BF16_SUBLANES = 16
VMEM_LIMIT = 56 * 1024 * 1024

NT_DIMS = (((1,), (1,)), ((), ()))


def _cparams(sem):
    return pltpu.CompilerParams(dimension_semantics=sem, vmem_limit_bytes=VMEM_LIMIT)


def _rms(x, g):
    return x * lax.rsqrt(jnp.mean(x * x, axis=-1, keepdims=True) + EPS) * g


def _split(x):
    hi = x.astype(BF16)
    lo = (x - hi.astype(F32)).astype(BF16)
    return hi, lo


def _dot(a, b):
    return jnp.dot(a, b, preferred_element_type=F32)


def _dot_nt(a, b):
    return lax.dot_general(a, b, NT_DIMS, preferred_element_type=F32)


def _log1p_exp_neg_abs(z):
    return jnp.log(1.0 + jnp.exp(-jnp.abs(z)))


def _cast_along(weights, grid):
    steps = 1
    for g in grid:
        steps *= g

    def step_index(*ids):
        flat = ids[0]
        for g, i in zip(grid[1:], ids[1:]):
            flat = flat * g + i
        return flat

    operands, in_specs, out_specs, out_shapes = [], [], [], []
    for arr, layer in weights:
        _, rows, cols = arr.shape
        slab = rows // steps
        assert slab * steps == rows and slab % BF16_SUBLANES == 0
        operands.append(arr)
        in_specs.append(pl.BlockSpec((None, slab, cols),
                                     lambda *ids, layer=layer: (layer, step_index(*ids), 0)))
        out_specs.append(pl.BlockSpec((slab, cols), lambda *ids: (step_index(*ids), 0)))
        out_shapes.append(jax.ShapeDtypeStruct((rows, cols), BF16))
    return operands, in_specs, out_specs, out_shapes


def _store_casts(src_refs, dst_refs):
    for src, dst in zip(src_refs, dst_refs):
        dst[...] = src[...].astype(dst.dtype)


def _mm_kernel(x_ref, w_ref, *refs, nt, scale, ncast):
    o_ref = refs[ncast]
    if nt:
        acc = _dot_nt(w_ref[...], x_ref[...])
    else:
        acc = _dot(x_ref[...], w_ref[...])
    if scale != 1.0:
        acc = acc * scale
    o_ref[...] = acc.astype(o_ref.dtype)
    _store_casts(refs[:ncast], refs[ncast + 1:])


def _mm_call(x, w, out_dtype, *, tm, tn, nt, j0, n, scale, casts, name):
    t, k = x.shape
    grid = (t // tm, n // tn)
    c_ops, c_in, c_out, c_shapes = _cast_along(casts, grid)
    if nt:
        w_spec = pl.BlockSpec((tn, k), lambda i, j: (j, 0))
        o_spec = pl.BlockSpec((tn, tm), lambda i, j: (j, i))
        o_shape = jax.ShapeDtypeStruct((n, t), out_dtype)
    else:
        w_spec = pl.BlockSpec((k, tn), lambda i, j: (0, j0 + j))
        o_spec = pl.BlockSpec((tm, tn), lambda i, j: (i, j))
        o_shape = jax.ShapeDtypeStruct((t, n), out_dtype)
    outs = pl.pallas_call(
        functools.partial(_mm_kernel, nt=nt, scale=scale, ncast=len(casts)),
        grid=grid,
        in_specs=[pl.BlockSpec((tm, k), lambda i, j: (i, 0)), w_spec] + c_in,
        out_specs=[o_spec] + c_out,
        out_shape=[o_shape] + c_shapes,
        compiler_params=_cparams(("parallel", "arbitrary")),
        name=name,
    )(x, w, *c_ops)
    return outs if casts else outs[0]


def _mm(x, w, out_dtype, *, tm, tn, scale=1.0, cols=None, casts=(), name="proj"):
    c0, n = cols if cols is not None else (0, w.shape[1])
    tn = min(tn, n)
    assert c0 % tn == 0 and n % tn == 0
    return _mm_call(x, w, out_dtype, tm=tm, tn=tn, nt=False, j0=c0 // tn, n=n, scale=scale,
                    casts=casts, name=name)


def _mm_t(x, wt, out_dtype, *, tm, tn, casts=(), name="proj_t"):
    n = wt.shape[0]
    return _mm_call(x, wt, out_dtype, tm=tm, tn=min(tn, n), nt=True, j0=0, n=n, scale=1.0,
                    casts=casts, name=name)


def _norm_mm_kernel(x_ref, g_ref, w_ref, *refs, ncast):
    o_ref, xn_ref = refs[ncast:ncast + 2]
    xn = _rms(x_ref[...], g_ref[...]).astype(xn_ref.dtype)
    xn_ref[...] = xn
    o_ref[...] = _dot(xn, w_ref[...]).astype(o_ref.dtype)
    _store_casts(refs[:ncast], refs[ncast + 2:])


def _norm_mm(x, g, w, out_dtype, *, tm, cols, casts=(), name):
    t, k = x.shape
    c0, n = cols
    assert c0 % n == 0
    grid = (t // tm,)
    c_ops, c_in, c_out, c_shapes = _cast_along(casts, grid)
    return pl.pallas_call(
        functools.partial(_norm_mm_kernel, ncast=len(casts)),
        grid=grid,
        in_specs=[pl.BlockSpec((tm, k), lambda i: (i, 0)),
                  pl.BlockSpec((1, k), lambda i: (0, 0)),
                  pl.BlockSpec((k, n), lambda i: (0, c0 // n))] + c_in,
        out_specs=[pl.BlockSpec((tm, n), lambda i: (i, 0)),
                   pl.BlockSpec((tm, k), lambda i: (i, 0))] + c_out,
        out_shape=[jax.ShapeDtypeStruct((t, n), out_dtype),
                   jax.ShapeDtypeStruct((t, k), BF16)] + c_shapes,
        compiler_params=_cparams(("parallel",)),
        name=name,
    )(x, g.reshape(1, k), w, *c_ops)


def _mm_res_norm_kernel(y_ref, w_ref, res_ref, g_ref, *refs, ncast):
    h_ref, xn_ref = refs[ncast:ncast + 2]
    h = res_ref[...] + _dot(y_ref[...], w_ref[...])
    h_ref[...] = h
    xn_ref[...] = _rms(h, g_ref[...]).astype(xn_ref.dtype)
    _store_casts(refs[:ncast], refs[ncast + 2:])


def _mm_res_norm(y, w, res, g, *, tm, casts=(), name):
    t, k = y.shape
    n = w.shape[1]
    grid = (t // tm,)
    c_ops, c_in, c_out, c_shapes = _cast_along(casts, grid)
    return pl.pallas_call(
        functools.partial(_mm_res_norm_kernel, ncast=len(casts)),
        grid=grid,
        in_specs=[pl.BlockSpec((tm, k), lambda i: (i, 0)),
                  pl.BlockSpec((k, n), lambda i: (0, 0)),
                  pl.BlockSpec((tm, n), lambda i: (i, 0)),
                  pl.BlockSpec((1, n), lambda i: (0, 0))] + c_in,
        out_specs=[pl.BlockSpec((tm, n), lambda i: (i, 0)),
                   pl.BlockSpec((tm, n), lambda i: (i, 0))] + c_out,
        out_shape=[jax.ShapeDtypeStruct((t, n), F32),
                   jax.ShapeDtypeStruct((t, n), BF16)] + c_shapes,
        compiler_params=_cparams(("parallel",)),
        name=name,
    )(y, w, res, g.reshape(1, n), *c_ops)


def _mlp_kernel(xn_ref, h_ref, wu_ref, wd_ref, g1_ref, g2_ref, acc_ref, *norm_refs, final):
    f = pl.program_id(1)

    @pl.when(f == 0)
    def _():
        acc_ref[...] = h_ref[...]

    u = _dot(xn_ref[...], wu_ref[...])
    a = jnp.square(jnp.maximum(u, 0.0)).astype(BF16)
    acc_ref[...] += _dot(a, wd_ref[...])

    @pl.when(f == pl.num_programs(1) - 1)
    def _():
        h = acc_ref[...]
        hn = h * lax.rsqrt(jnp.mean(h * h, axis=-1, keepdims=True) + EPS)
        if final:
            acc_ref[...] = hn * g1_ref[...]
        else:
            xa_ref, xb_ref = norm_refs
            xa_ref[...] = (hn * g1_ref[...]).astype(BF16)
            xb_ref[...] = (hn * g2_ref[...]).astype(BF16)


def _mlp(xn, h, wu, wd, g1, g2, *, tm, tf, final, name):
    t, d = xn.shape
    ff = wu.shape[1]
    row = pl.BlockSpec((tm, d), lambda i, f: (i, 0))
    vec = pl.BlockSpec((1, d), lambda i, f: (0, 0))
    if final:
        out_specs = row
        out_shape = jax.ShapeDtypeStruct((t, d), F32)
    else:
        out_specs = [row, row, row]
        out_shape = [jax.ShapeDtypeStruct((t, d), F32),
                     jax.ShapeDtypeStruct((t, d), BF16),
                     jax.ShapeDtypeStruct((t, d), BF16)]
    return pl.pallas_call(
        functools.partial(_mlp_kernel, final=final),
        grid=(t // tm, ff // tf),
        in_specs=[row, row,
                  pl.BlockSpec((d, tf), lambda i, f: (0, f)),
                  pl.BlockSpec((tf, d), lambda i, f: (f, 0)),
                  vec, vec],
        out_specs=out_specs,
        out_shape=out_shape,
        compiler_params=_cparams(("parallel", "arbitrary")),
        name=name,
    )(xn, h, wu, wd, g1.reshape(1, d), g2.reshape(1, d))


def _gla_kernel(q_ref, k_ref, vt_ref, r_ref, a_ref, a_next_ref, wg_ref, bg_ref, gh_ref,
                y_ref, st_ref, cum2_ref, mild_ref):
    c = GLA_CHUNK
    nd = GLA_DIAG
    step = pl.program_id(0)
    slot = step % 2
    cum_ref = cum2_ref.at[slot]
    ri = lax.broadcasted_iota(jnp.int32, (c, c), 0)
    ci = lax.broadcasted_iota(jnp.int32, (c, c), 1)

    def gate(a, dst):
        z = _dot(a, wg_ref[...]) + bg_ref[...]
        log_alpha = (jnp.minimum(z, 0.0) - _log1p_exp_neg_abs(z)) * (1.0 / GLA_TAU)
        tri = (ci <= ri).astype(BF16)
        la_hi, la_lo = _split(log_alpha)
        cum = _dot(tri, la_hi) + _dot(tri, la_lo)
        cum2_ref[dst] = cum
        mild_ref[dst] = (jnp.max(-cum[c - 1:c, :]) <= GLA_SAFE_SPAN).astype(jnp.int32)

    @pl.when(step == 0)
    def _():
        st_ref[...] = jnp.zeros_like(st_ref)
        gate(a_ref[...], 0)

    row_k = lax.broadcasted_iota(jnp.int32, (c, GLA_DK), 0)
    row_d = lax.broadcasted_iota(jnp.int32, (nd, GLA_DK), 0)
    lane_d = lax.broadcasted_iota(jnp.int32, (nd, c), 1)

    def scores_one_product(ks, ch, kh, qs, qe):
        k_inv = (kh * jnp.exp(-ch)).astype(BF16)
        return jnp.where(ci <= ri, _dot_nt(qe, k_inv), 0.0)

    def scores_any_decay(ks, ch, kh, qs, qe):
        p = jnp.zeros((c, c), F32)
        for b in GLA_LEVELS:
            upper = (row_k % (2 * b)) >= b
            same_group = (ri // (2 * b)) == (ci // (2 * b))
            ref = jnp.concatenate(
                [jnp.broadcast_to(cum_ref[g + b:g + b + 1, ks], (2 * b, GLA_DK))
                 for g in range(0, c, 2 * b)], axis=0)
            e = jnp.exp(-jnp.abs(ch - ref))
            q_sc = jnp.where(upper, qs * e, 0.0).astype(BF16)
            k_sc = jnp.where(upper, 0.0, kh * e).astype(BF16)
            p = p + jnp.where(same_group, _dot_nt(q_sc, k_sc), 0.0)
        p_rows = []
        for r0 in range(0, c, nd):
            c_i = ch[r0:r0 + nd]
            q_i = qs[r0:r0 + nd]
            p_i = p[r0:r0 + nd]
            for j in range(nd):
                c_j = cum_ref[r0 + j:r0 + j + 1, ks]
                k_j = k_ref[r0 + j:r0 + j + 1, ks]
                dec = jnp.exp(jnp.where(row_d >= j, c_i - c_j, -jnp.inf))
                col = jnp.sum(q_i * k_j * dec, axis=1, keepdims=True)
                p_i = jnp.where(lane_d == r0 + j, col, p_i)
            p_rows.append(p_i)
        return jnp.concatenate(p_rows, axis=0)

    def all_heads(scores):
        for h in range(GLA_HEADS):
            ks = slice(h * GLA_DK, (h + 1) * GLA_DK)
            vs = slice(h * GLA_DV, (h + 1) * GLA_DV)
            ch = cum_ref[:, ks]
            kh = k_ref[:, ks]
            qs = q_ref[:, ks] * (GLA_DK ** -0.5)
            vt = vt_ref[vs, :]
            st = st_ref[h]

            qe = (qs * jnp.exp(ch)).astype(BF16)
            o = _dot_nt(qe, st.astype(BF16))
            p = scores(ks, ch, kh, qs, qe)
            o = o + _dot_nt(p.astype(BF16), vt)

            last = cum_ref[c - 1:c, ks]
            kd = (kh * jnp.exp(last - ch)).astype(BF16)
            st_ref[h] = st * jnp.exp(last) + _dot(vt, kd)

            on = _rms(o, gh_ref[...])
            rr = r_ref[:, vs].astype(F32)
            y_ref[:, vs] = (on * (rr * jax.nn.sigmoid(rr))).astype(y_ref.dtype)
        gate(a_next_ref[...], 1 - slot)

    mild = mild_ref[slot] != 0

    @pl.when(mild)
    def _():
        all_heads(scores_one_product)

    @pl.when(jnp.logical_not(mild))
    def _():
        all_heads(scores_any_decay)


def _gla(q, k, vt, r, a, wg, bg, gh):
    t = q.shape[0]
    c = GLA_CHUNK
    const = lambda shape: pl.BlockSpec(shape, lambda i: (0, 0))
    last = t // c - 1
    return pl.pallas_call(
        _gla_kernel,
        grid=(t // c,),
        in_specs=[pl.BlockSpec((c, GLA_QK), lambda i: (i, 0)),
                  pl.BlockSpec((c, GLA_QK), lambda i: (i, 0)),
                  pl.BlockSpec((GLA_V, c), lambda i: (0, i)),
                  pl.BlockSpec((c, GLA_V), lambda i: (i, 0)),
                  pl.BlockSpec((c, LANES), lambda i: (i, 0)),
                  pl.BlockSpec((c, LANES), lambda i: (jnp.minimum(i + 1, last), 0)),
                  const((LANES, GLA_QK)),
                  const((1, GLA_QK)),
                  const((1, GLA_DV))],
        out_specs=pl.BlockSpec((c, GLA_V), lambda i: (i, 0)),
        out_shape=jax.ShapeDtypeStruct((t, GLA_V), BF16),
        scratch_shapes=[pltpu.VMEM((GLA_HEADS, GLA_DV, GLA_DK), F32),
                        pltpu.VMEM((2, c, GLA_QK), F32),
                        pltpu.SMEM((2,), jnp.int32)],
        compiler_params=_cparams(("arbitrary",)),
        name="gla_scan",
    )(q, k, vt, r, a, a, wg.astype(BF16), bg.reshape(1, GLA_QK), gh.reshape(1, GLA_DV))


def _sb_sum_matrix():
    n = SB_FUSED * SB_TK
    j = jnp.arange(n)[:, None]
    s = jnp.arange(n)[None, :]
    return -jnp.concatenate([j > s, jnp.ones((n, LANES), bool)], axis=1).astype(BF16)


def _sb_kernel(q_ref, k_ref, v_ref, u_ref, o_ref, acc_ref, carry_ref):
    tq, tk = SB_TQ, SB_TK
    rows = SB_GROUP * tq
    nf = SB_FUSED
    win = nf * tk
    step = pl.program_id(1)

    def causal(width, first_visible):
        qpos = lax.broadcasted_iota(jnp.int32, (rows, width), 0) % tq
        kpos = lax.broadcasted_iota(jnp.int32, (rows, width), 1)
        return kpos < qpos + first_visible

    blocks = []
    for s in range(SB_QB):
        qs = jnp.concatenate([q_ref[s * tq:(s + 1) * tq, g * SB_DH:(g + 1) * SB_DH]
                              for g in range(SB_GROUP)], axis=0)
        blocks.append((s, step * SB_QB + s, qs))

    def tile(s, qs, kt, masked):
        k0 = pl.multiple_of(kt * tk, tk)
        z = _dot_nt(qs, k_ref[pl.ds(k0, tk), :])
        sp = jnp.maximum(z, 0.0) + _log1p_exp_neg_abs(z)
        visible = causal(tk, 0)
        spm = jnp.where(visible, sp, 0.0) if masked else sp
        spb = spm.astype(BF16)
        carry = carry_ref[s]
        tail = _dot(spb, u_ref[:tk, :tk]) + carry[:, :tk]
        w = jnp.exp((z - sp) + tail)
        if masked:
            w = jnp.where(visible, w, 0.0)
        acc_ref[s] += _dot(w.astype(BF16), v_ref[pl.ds(k0, tk), :])
        carry = carry + _dot(spb, u_ref[:tk, win:])
        carry_ref[s] = carry
        return jnp.max(carry)

    def first_fused(s, qi, qs):
        k0 = pl.multiple_of((qi - (nf - 1)) * tk, tk)
        z = _dot_nt(qs, k_ref[pl.ds(k0, win), :])
        sp = jnp.maximum(z, 0.0) + _log1p_exp_neg_abs(z)
        visible = causal(win, (nf - 1) * tk)
        spb = jnp.where(visible, sp, 0.0).astype(BF16)
        sums = _dot(spb, u_ref[...])
        w = jnp.where(visible, jnp.exp((z - sp) + sums[:, :win]), 0.0)
        acc_ref[s] = _dot(w.astype(BF16), v_ref[pl.ds(k0, win), :])
        carry_ref[s] = sums[:, win:]
        return qi - nf, jnp.max(sums[:, win:])

    def first_single(s, qi, qs):
        acc_ref[s] = jnp.zeros((rows, SB_DH), F32)
        carry_ref[s] = jnp.zeros((rows, LANES), F32)
        return qi - 1, tile(s, qs, qi, True)

    def all_fused():
        return tuple(first_fused(*blk) for blk in blocks)

    def all_single():
        return tuple(first_single(*blk) for blk in blocks)

    starts = lax.cond(step * SB_QB >= nf - 1, all_fused, all_single)

    def unfinished(state):
        kt, mx = state
        return jnp.logical_and(kt >= 0, mx > SB_EXIT)

    for (s, _, qs), start in zip(blocks, starts):
        def earlier_tile(state, s=s, qs=qs):
            kt, _ = state
            return kt - 1, tile(s, qs, kt, False)

        lax.while_loop(unfinished, earlier_tile, start)
        for g in range(SB_GROUP):
            o_ref[s * tq:(s + 1) * tq, g * SB_DH:(g + 1) * SB_DH] = (
                acc_ref[s, g * tq:(g + 1) * tq, :].astype(o_ref.dtype))


def _sb_attention(q, kv):
    t = q.shape[0]
    tq = SB_TQ * SB_QB
    gw = SB_GROUP * SB_DH
    u = _sb_sum_matrix()
    return pl.pallas_call(
        _sb_kernel,
        grid=(SB_KVH, t // tq),
        in_specs=[pl.BlockSpec((tq, gw), lambda n, i: (i, n)),
                  pl.BlockSpec((t, SB_DH), lambda n, i: (0, n)),
                  pl.BlockSpec((t, SB_DH), lambda n, i: (0, SB_KVH + n)),
                  pl.BlockSpec(u.shape, lambda n, i: (0, 0))],
        out_specs=pl.BlockSpec((tq, gw), lambda n, i: (i, n)),
        out_shape=jax.ShapeDtypeStruct((t, SB_HEADS * SB_DH), BF16),
        scratch_shapes=[pltpu.VMEM((SB_QB, SB_GROUP * SB_TQ, SB_DH), F32),
                        pltpu.VMEM((SB_QB, SB_GROUP * SB_TQ, LANES), F32)],
        compiler_params=_cparams(("arbitrary", "arbitrary")),
        name="stick_breaking",
    )(q, kv, kv, u)


def _row_tile(t, want):
    return want if t % want == 0 else t


def kernel(x, norm1_g, norm2_g, gla_w_in, gla_w_gate_up, gla_b_gate, gla_head_g, gla_w_out,
           kv_norm_g, kv_w, sb_w_q, sb_w_out, mlp_w_up, mlp_w_down, final_g):
    b, t, d = x.shape
    assert b == 1 and d == D_MODEL and t % GLA_CHUNK == 0
    x2 = x.reshape(t, d)
    tm_big = _row_tile(t, 1024)
    tn = 1024
    tm = _row_tile(t, 512)
    tf = 1024

    w_in = gla_w_in[0].astype(BF16)
    o_k = GLA_QK
    o_v = 2 * GLA_QK
    o_r = o_v + GLA_V
    o_a = o_r + GLA_V
    w_vt = w_in[:, o_v:o_r].T
    w_a = jnp.pad(w_in[:, o_a:], ((0, 0), (0, LANES - GLA_RANK)))
    w_g = jnp.pad(gla_w_gate_up[0], ((0, LANES - GLA_RANK), (0, 0)))

    kv_w3 = kv_w.reshape(1, *kv_w.shape)

    q, xn, w_out0 = _norm_mm(x2, norm1_g[0], w_in, F32, tm=tm_big, cols=(0, GLA_QK),
                             casts=[(gla_w_out, 0)], name="gla_q")
    k, w_q1 = _mm(xn, w_in, F32, tm=tm_big, tn=tn, cols=(o_k, GLA_QK),
                  casts=[(sb_w_q, 0)], name="gla_k")
    vt, w_up0 = _mm_t(xn, w_vt, BF16, tm=tm_big, tn=tn, casts=[(mlp_w_up, 0)], name="gla_vt")
    r, w_down0 = _mm(xn, w_in, BF16, tm=tm_big, tn=tn, cols=(o_r, GLA_V),
                     casts=[(mlp_w_down, 0)], name="gla_r")
    a = _mm(xn, w_a, BF16, tm=tm_big, tn=LANES, name="gla_a")
    y = _gla(q, k, vt, r, a, w_g, gla_b_gate[0], gla_head_g[0])
    h, xn, w_out1, w_kv = _mm_res_norm(y, w_out0, x2, norm2_g[0], tm=tm,
                                       casts=[(sb_w_out, 0), (kv_w3, 0)], name="gla_out")
    h, xn_q, xn_kv = _mlp(xn, h, w_up0, w_down0, norm1_g[1], kv_norm_g,
                          tm=tm, tf=tf, final=False, name="mlp0")

    q2, w_up1 = _mm(xn_q, w_q1, BF16, tm=tm_big, tn=tn, scale=SB_DH ** -0.5,
                    casts=[(mlp_w_up, 1)], name="sb_q")
    kv = _mm(xn_kv, w_kv, BF16, tm=tm_big, tn=tn, name="shared_kv")
    o = _sb_attention(q2, kv)
    h, xn, w_down1 = _mm_res_norm(o, w_out1, h, norm2_g[1], tm=tm,
                                  casts=[(mlp_w_down, 1)], name="sb_out")
    out = _mlp(xn, h, w_up1, w_down1, final_g, final_g, tm=tm, tf=tf, final=True, name="mlp1")
    return out.reshape(b, t, d)
```

```python
import functools

import jax
import jax.numpy as jnp
from jax import lax
from jax.experimental import pallas as pl
from jax.experimental.pallas import tpu as pltpu

F32 = jnp.float32
BF16 = jnp.bfloat16

EPS = 1e-6
D_MODEL = 2048
D_FF = 4 * D_MODEL

GLA_HEADS = 4
GLA_DK = 256
GLA_DV = 512
GLA_QK = GLA_HEADS * GLA_DK
GLA_V = GLA_HEADS * GLA_DV
GLA_RANK = 16
GLA_TAU = 16.0
GLA_CHUNK = 128
GLA_DIAG = 8
GLA_LEVELS = (64, 32, 16, 8)
GLA_SAFE_SPAN = 60.0

SB_HEADS = 16
SB_DH = 128
SB_KVH = 4
SB_GROUP = SB_HEADS // SB_KVH
SB_KV = SB_KVH * SB_DH
SB_TQ = 128
SB_TK = 128
SB_FUSED = 3
SB_QB = 2
SB_EXIT = -110.0

LANES = 128
BF16_SUBLANES = 16
VMEM_LIMIT = 56 * 1024 * 1024

NT_DIMS = (((1,), (1,)), ((), ()))
TN_NT_DIMS = (((0,), (1,)), ((), ()))


def _cparams(sem):
    return pltpu.CompilerParams(dimension_semantics=sem, vmem_limit_bytes=VMEM_LIMIT)


def _rms(x, g):
    return x * lax.rsqrt(jnp.mean(x * x, axis=-1, keepdims=True) + EPS) * g


def _split(x):
    hi = x.astype(BF16)
    lo = (x - hi.astype(F32)).astype(BF16)
    return hi, lo


def _dot(a, b):
    return jnp.dot(a, b, preferred_element_type=F32)


def _dot_nt(a, b):
    return lax.dot_general(a, b, NT_DIMS, preferred_element_type=F32)


def _log1p_exp_neg_abs(z):
    return jnp.log(1.0 + jnp.exp(-jnp.abs(z)))


def _cast_along(weights, grid):
    steps = 1
    for g in grid:
        steps *= g

    def step_index(*ids):
        flat = ids[0]
        for g, i in zip(grid[1:], ids[1:]):
            flat = flat * g + i
        return flat

    operands, in_specs, out_specs, out_shapes = [], [], [], []
    for arr, layer in weights:
        _, rows, cols = arr.shape
        slab = rows // steps
        assert slab * steps == rows and slab % BF16_SUBLANES == 0
        operands.append(arr)
        in_specs.append(pl.BlockSpec((None, slab, cols),
                                     lambda *ids, layer=layer: (layer, step_index(*ids), 0)))
        out_specs.append(pl.BlockSpec((slab, cols), lambda *ids: (step_index(*ids), 0)))
        out_shapes.append(jax.ShapeDtypeStruct((rows, cols), BF16))
    return operands, in_specs, out_specs, out_shapes


def _store_casts(src_refs, dst_refs):
    for src, dst in zip(src_refs, dst_refs):
        dst[...] = src[...].astype(dst.dtype)


def _mm_kernel(x_ref, w_ref, *refs, nt, scale, ncast):
    o_ref = refs[ncast]
    if nt:
        acc = lax.dot_general(w_ref[...], x_ref[...], TN_NT_DIMS,
                              preferred_element_type=F32)
    else:
        acc = _dot(x_ref[...], w_ref[...])
    if scale != 1.0:
        acc = acc * scale
    o_ref[...] = acc.astype(o_ref.dtype)
    _store_casts(refs[:ncast], refs[ncast + 1:])


def _mm_call(x, w, out_dtype, *, tm, tn, nt, j0, n, scale, casts, name):
    t, k = x.shape
    grid = (t // tm, n // tn)
    c_ops, c_in, c_out, c_shapes = _cast_along(casts, grid)
    w_spec = pl.BlockSpec((k, tn), lambda i, j: (0, j0 + j))
    if nt:
        o_spec = pl.BlockSpec((tn, tm), lambda i, j: (j, i))
        o_shape = jax.ShapeDtypeStruct((n, t), out_dtype)
    else:
        o_spec = pl.BlockSpec((tm, tn), lambda i, j: (i, j))
        o_shape = jax.ShapeDtypeStruct((t, n), out_dtype)
    outs = pl.pallas_call(
        functools.partial(_mm_kernel, nt=nt, scale=scale, ncast=len(casts)),
        grid=grid,
        in_specs=[pl.BlockSpec((tm, k), lambda i, j: (i, 0)), w_spec] + c_in,
        out_specs=[o_spec] + c_out,
        out_shape=[o_shape] + c_shapes,
        compiler_params=_cparams(("parallel", "arbitrary")),
        name=name,
    )(x, w, *c_ops)
    return outs if casts else outs[0]


def _mm(x, w, out_dtype, *, tm, tn, scale=1.0, cols=None, casts=(), transposed=False, name="proj"):
    c0, n = cols if cols is not None else (0, w.shape[1])
    tn = min(tn, n)
    assert c0 % tn == 0 and n % tn == 0
    return _mm_call(x, w, out_dtype, tm=tm, tn=tn, nt=transposed, j0=c0 // tn, n=n, scale=scale,
                    casts=casts, name=name)


def _norm_mm_kernel(x_ref, g_ref, w_ref, *refs, ncast):
    o_ref, xn_ref = refs[ncast:ncast + 2]
    xn = _rms(x_ref[...], g_ref[...]).astype(xn_ref.dtype)
    xn_ref[...] = xn
    o_ref[...] = _dot(xn, w_ref[...]).astype(o_ref.dtype)
    _store_casts(refs[:ncast], refs[ncast + 2:])


def _norm_mm(x, g, w, out_dtype, *, tm, cols, casts=(), name):
    t, k = x.shape
    c0, n = cols
    assert c0 % n == 0
    grid = (t // tm,)
    c_ops, c_in, c_out, c_shapes = _cast_along(casts, grid)
    return pl.pallas_call(
        functools.partial(_norm_mm_kernel, ncast=len(casts)),
        grid=grid,
        in_specs=[pl.BlockSpec((tm, k), lambda i: (i, 0)),
                  pl.BlockSpec((1, k), lambda i: (0, 0)),
                  pl.BlockSpec((k, n), lambda i: (0, c0 // n))] + c_in,
        out_specs=[pl.BlockSpec((tm, n), lambda i: (i, 0)),
                   pl.BlockSpec((tm, k), lambda i: (i, 0))] + c_out,
        out_shape=[jax.ShapeDtypeStruct((t, n), out_dtype),
                   jax.ShapeDtypeStruct((t, k), BF16)] + c_shapes,
        compiler_params=_cparams(("parallel",)),
        name=name,
    )(x, g.reshape(1, k), w, *c_ops)


def _mm_res_norm_kernel(y_ref, w_ref, res_ref, g_ref, *refs, ncast):
    h_ref, xn_ref = refs[ncast:ncast + 2]
    h = res_ref[...] + _dot(y_ref[...], w_ref[...])
    h_ref[...] = h
    xn_ref[...] = _rms(h, g_ref[...]).astype(xn_ref.dtype)
    _store_casts(refs[:ncast], refs[ncast + 2:])


def _mm_res_norm(y, w, res, g, *, tm, casts=(), name):
    t, k = y.shape
    n = w.shape[1]
    grid = (t // tm,)
    c_ops, c_in, c_out, c_shapes = _cast_along(casts, grid)
    return pl.pallas_call(
        functools.partial(_mm_res_norm_kernel, ncast=len(casts)),
        grid=grid,
        in_specs=[pl.BlockSpec((tm, k), lambda i: (i, 0)),
                  pl.BlockSpec((k, n), lambda i: (0, 0)),
                  pl.BlockSpec((tm, n), lambda i: (i, 0)),
                  pl.BlockSpec((1, n), lambda i: (0, 0))] + c_in,
        out_specs=[pl.BlockSpec((tm, n), lambda i: (i, 0)),
                   pl.BlockSpec((tm, n), lambda i: (i, 0))] + c_out,
        out_shape=[jax.ShapeDtypeStruct((t, n), F32),
                   jax.ShapeDtypeStruct((t, n), BF16)] + c_shapes,
        compiler_params=_cparams(("parallel",)),
        name=name,
    )(y, w, res, g.reshape(1, n), *c_ops)


def _mlp_kernel(xn_ref, h_ref, wu_ref, wd_ref, g1_ref, g2_ref, *refs, final, ncast):
    n_out = 1 if final else 3
    acc_ref = refs[ncast]
    norm_refs = refs[ncast + 1:ncast + n_out]
    _store_casts(refs[:ncast], refs[ncast + n_out:])
    f = pl.program_id(1)

    @pl.when(f == 0)
    def _():
        acc_ref[...] = h_ref[...]

    u = _dot(xn_ref[...], wu_ref[...])
    a = jnp.square(jnp.maximum(u, 0.0)).astype(BF16)
    acc_ref[...] += _dot(a, wd_ref[...])

    @pl.when(f == pl.num_programs(1) - 1)
    def _():
        h = acc_ref[...]
        hn = h * lax.rsqrt(jnp.mean(h * h, axis=-1, keepdims=True) + EPS)
        if final:
            acc_ref[...] = hn * g1_ref[...]
        else:
            xa_ref, xb_ref = norm_refs
            xa_ref[...] = (hn * g1_ref[...]).astype(BF16)
            xb_ref[...] = (hn * g2_ref[...]).astype(BF16)


def _mlp(xn, h, wu, wd, g1, g2, *, tm, tf, final, casts=(), name):
    t, d = xn.shape
    ff = wu.shape[1]
    grid = (t // tm, ff // tf)
    c_ops, c_in, c_out, c_shapes = _cast_along(casts, grid)
    row = pl.BlockSpec((tm, d), lambda i, f: (i, 0))
    vec = pl.BlockSpec((1, d), lambda i, f: (0, 0))
    out_specs = [row]
    out_shape = [jax.ShapeDtypeStruct((t, d), F32)]
    if not final:
        out_specs += [row, row]
        out_shape += [jax.ShapeDtypeStruct((t, d), BF16), jax.ShapeDtypeStruct((t, d), BF16)]
    return pl.pallas_call(
        functools.partial(_mlp_kernel, final=final, ncast=len(casts)),
        grid=grid,
        in_specs=[row, row,
                  pl.BlockSpec((d, tf), lambda i, f: (0, f)),
                  pl.BlockSpec((tf, d), lambda i, f: (f, 0)),
                  vec, vec] + c_in,
        out_specs=out_specs + c_out,
        out_shape=out_shape + c_shapes,
        compiler_params=_cparams(("parallel", "arbitrary")),
        name=name,
    )(xn, h, wu, wd, g1.reshape(1, d), g2.reshape(1, d), *c_ops)


def _gla_kernel(q_ref, k_ref, vt_ref, r_ref, a_ref, a_next_ref, wg_ref, bg_ref, gh_ref,
                y_ref, st_ref, cum2_ref, mild_ref):
    c = GLA_CHUNK
    nd = GLA_DIAG
    step = pl.program_id(0)
    slot = step % 2
    cum_ref = cum2_ref.at[slot]
    ri = lax.broadcasted_iota(jnp.int32, (c, c), 0)
    ci = lax.broadcasted_iota(jnp.int32, (c, c), 1)

    def gate(a, dst):
        z = _dot(a, wg_ref[...]) + bg_ref[...]
        log_alpha = (jnp.minimum(z, 0.0) - _log1p_exp_neg_abs(z)) * (1.0 / GLA_TAU)
        tri = (ci <= ri).astype(BF16)
        la_hi, la_lo = _split(log_alpha)
        cum = _dot(tri, la_hi) + _dot(tri, la_lo)
        cum2_ref[dst] = cum
        mild_ref[dst] = (jnp.max(-cum[c - 1:c, :]) <= GLA_SAFE_SPAN).astype(jnp.int32)

    @pl.when(step == 0)
    def _():
        st_ref[...] = jnp.zeros_like(st_ref)
        gate(a_ref[...], 0)

    row_k = lax.broadcasted_iota(jnp.int32, (c, GLA_DK), 0)
    row_d = lax.broadcasted_iota(jnp.int32, (nd, GLA_DK), 0)
    lane_d = lax.broadcasted_iota(jnp.int32, (nd, c), 1)

    def scores_one_product(ks, ch, kh, qs, qe):
        k_inv = (kh * jnp.exp(-ch)).astype(BF16)
        return jnp.where(ci <= ri, _dot_nt(qe, k_inv), 0.0)

    def scores_any_decay(ks, ch, kh, qs, qe):
        p = jnp.zeros((c, c), F32)
        for b in GLA_LEVELS:
            upper = (row_k % (2 * b)) >= b
            same_group = (ri // (2 * b)) == (ci // (2 * b))
            ref = jnp.concatenate(
                [jnp.broadcast_to(cum_ref[g + b:g + b + 1, ks], (2 * b, GLA_DK))
                 for g in range(0, c, 2 * b)], axis=0)
            e = jnp.exp(-jnp.abs(ch - ref))
            q_sc = jnp.where(upper, qs * e, 0.0).astype(BF16)
            k_sc = jnp.where(upper, 0.0, kh * e).astype(BF16)
            p = p + jnp.where(same_group, _dot_nt(q_sc, k_sc), 0.0)
        p_rows = []
        for r0 in range(0, c, nd):
            c_i = ch[r0:r0 + nd]
            q_i = qs[r0:r0 + nd]
            p_i = p[r0:r0 + nd]
            for j in range(nd):
                c_j = cum_ref[r0 + j:r0 + j + 1, ks]
                k_j = k_ref[r0 + j:r0 + j + 1, ks]
                dec = jnp.exp(jnp.where(row_d >= j, c_i - c_j, -jnp.inf))
                col = jnp.sum(q_i * k_j * dec, axis=1, keepdims=True)
                p_i = jnp.where(lane_d == r0 + j, col, p_i)
            p_rows.append(p_i)
        return jnp.concatenate(p_rows, axis=0)

    def all_heads(scores):
        for h in range(GLA_HEADS):
            ks = slice(h * GLA_DK, (h + 1) * GLA_DK)
            vs = slice(h * GLA_DV, (h + 1) * GLA_DV)
            ch = cum_ref[:, ks]
            kh = k_ref[:, ks]
            qs = q_ref[:, ks] * (GLA_DK ** -0.5)
            vt = vt_ref[vs, :]
            st = st_ref[h]

            qe = (qs * jnp.exp(ch)).astype(BF16)
            o = _dot_nt(qe, st.astype(BF16))
            p = scores(ks, ch, kh, qs, qe)
            o = o + _dot_nt(p.astype(BF16), vt)

            last = cum_ref[c - 1:c, ks]
            kd = (kh * jnp.exp(last - ch)).astype(BF16)
            st_ref[h] = st * jnp.exp(last) + _dot(vt, kd)

            on = _rms(o, gh_ref[...])
            rr = r_ref[:, vs].astype(F32)
            y_ref[:, vs] = (on * (rr * jax.nn.sigmoid(rr))).astype(y_ref.dtype)
        gate(a_next_ref[...], 1 - slot)

    mild = mild_ref[slot] != 0

    @pl.when(mild)
    def _():
        all_heads(scores_one_product)

    @pl.when(jnp.logical_not(mild))
    def _():
        all_heads(scores_any_decay)


def _gla(q, k, vt, r, a, wg, bg, gh):
    t = q.shape[0]
    c = GLA_CHUNK
    const = lambda shape: pl.BlockSpec(shape, lambda i: (0, 0))
    last = t // c - 1
    return pl.pallas_call(
        _gla_kernel,
        grid=(t // c,),
        in_specs=[pl.BlockSpec((c, GLA_QK), lambda i: (i, 0)),
                  pl.BlockSpec((c, GLA_QK), lambda i: (i, 0)),
                  pl.BlockSpec((GLA_V, c), lambda i: (0, i)),
                  pl.BlockSpec((c, GLA_V), lambda i: (i, 0)),
                  pl.BlockSpec((c, LANES), lambda i: (i, 0)),
                  pl.BlockSpec((c, LANES), lambda i: (jnp.minimum(i + 1, last), 0)),
                  const((LANES, GLA_QK)),
                  const((1, GLA_QK)),
                  const((1, GLA_DV))],
        out_specs=pl.BlockSpec((c, GLA_V), lambda i: (i, 0)),
        out_shape=jax.ShapeDtypeStruct((t, GLA_V), BF16),
        scratch_shapes=[pltpu.VMEM((GLA_HEADS, GLA_DV, GLA_DK), F32),
                        pltpu.VMEM((2, c, GLA_QK), F32),
                        pltpu.SMEM((2,), jnp.int32)],
        compiler_params=_cparams(("arbitrary",)),
        name="gla_scan",
    )(q, k, vt, r, a, a, wg.astype(BF16), bg.reshape(1, GLA_QK), gh.reshape(1, GLA_DV))


def _sb_sum_matrix():
    tk = SB_TK
    n = SB_FUSED * tk
    j = jnp.arange(n)[:, None]
    s = jnp.arange(n)[None, :]
    later = (j > s)
    cols = [later[:, :tk], jnp.ones((n, tk), bool)] + [later[:, i * tk:(i + 1) * tk] for i in range(1, SB_FUSED)]
    return -jnp.concatenate(cols, axis=1).astype(BF16)


def _sb_kernel(q_ref, kt_ref, v_ref, u_ref, o_ref, acc_ref, carry_ref):
    tq, tk = SB_TQ, SB_TK
    rows = SB_GROUP * tq
    nf = SB_FUSED
    step = pl.program_id(1)
    qpos = lax.broadcasted_iota(jnp.int32, (rows, tk), 0) % tq
    kpos = lax.broadcasted_iota(jnp.int32, (rows, tk), 1)
    causal = kpos < qpos

    blocks = []
    for s in range(SB_QB):
        qs = jnp.concatenate([q_ref[s * tq:(s + 1) * tq, g * SB_DH:(g + 1) * SB_DH]
                              for g in range(SB_GROUP)], axis=0)
        blocks.append((s, step * SB_QB + s, qs))

    def tile(s, qs, kt, masked):
        k0 = pl.multiple_of(kt * tk, tk)
        z = _dot(qs, kt_ref[:, pl.ds(k0, tk)])
        sp = jnp.maximum(z, 0.0) + _log1p_exp_neg_abs(z)
        spm = jnp.where(causal, sp, 0.0) if masked else sp
        spb = spm.astype(BF16)
        tot = _dot(spb, u_ref[(nf - 1) * tk:, tk:2 * tk])
        tail = _dot(spb, u_ref[(nf - 1) * tk:, nf * tk:]) + carry_ref[s]
        w = jnp.exp((z - sp) + tail)
        if masked:
            w = jnp.where(causal, w, 0.0)
        acc_ref[s] += _dot(w.astype(BF16), v_ref[pl.ds(k0, tk), :])
        carry = carry_ref[s] + tot
        carry_ref[s] = carry
        return jnp.max(carry)

    def first_fused(s, qi, qs):
        k0 = pl.multiple_of((qi - (nf - 1)) * tk, tk)
        z = _dot(qs, kt_ref[:, pl.ds(k0, nf * tk)])
        sp = jnp.maximum(z, 0.0) + _log1p_exp_neg_abs(z)
        spm = jnp.concatenate([sp[:, :(nf - 1) * tk],
                               jnp.where(causal, sp[:, (nf - 1) * tk:], 0.0)], axis=1)
        spb = spm.astype(BF16)
        first = _dot(spb, u_ref[:, :2 * tk])
        later = _dot(spb[:, tk:], u_ref[tk:, 2 * tk:])
        tot = first[:, tk:]
        w = jnp.exp((z - sp) + jnp.concatenate([first[:, :tk], later], axis=1))
        w = jnp.concatenate([w[:, :(nf - 1) * tk],
                             jnp.where(causal, w[:, (nf - 1) * tk:], 0.0)], axis=1)
        acc_ref[s] = _dot(w.astype(BF16), v_ref[pl.ds(k0, nf * tk), :])
        carry_ref[s] = tot
        return qi - nf, jnp.max(tot)

    def first_single(s, qi, qs):
        acc_ref[s] = jnp.zeros((rows, SB_DH), F32)
        carry_ref[s] = jnp.zeros((rows, tk), F32)
        return qi - 1, tile(s, qs, qi, True)

    def all_fused():
        return tuple(first_fused(*blk) for blk in blocks)

    def all_single():
        return tuple(first_single(*blk) for blk in blocks)

    starts = lax.cond(step * SB_QB >= nf - 1, all_fused, all_single)

    def unfinished(state):
        kt, mx = state
        return jnp.logical_and(kt >= 0, mx > SB_EXIT)

    for (s, _, qs), start in zip(blocks, starts):
        def earlier_tile(state, s=s, qs=qs):
            kt, _ = state
            return kt - 1, tile(s, qs, kt, False)

        lax.while_loop(unfinished, earlier_tile, start)
        for g in range(SB_GROUP):
            o_ref[s * tq:(s + 1) * tq, g * SB_DH:(g + 1) * SB_DH] = (
                acc_ref[s, g * tq:(g + 1) * tq, :].astype(o_ref.dtype))


def _sb_attention(q, kt, v):
    t = q.shape[0]
    tq = SB_TQ * SB_QB
    gw = SB_GROUP * SB_DH
    u = _sb_sum_matrix()
    return pl.pallas_call(
        _sb_kernel,
        grid=(SB_KVH, t // tq),
        in_specs=[pl.BlockSpec((tq, gw), lambda n, i: (i, n)),
                  pl.BlockSpec((SB_DH, t), lambda n, i: (n, 0)),
                  pl.BlockSpec((t, SB_DH), lambda n, i: (0, n)),
                  pl.BlockSpec(u.shape, lambda n, i: (0, 0))],
        out_specs=pl.BlockSpec((tq, gw), lambda n, i: (i, n)),
        out_shape=jax.ShapeDtypeStruct((t, SB_HEADS * SB_DH), BF16),
        scratch_shapes=[pltpu.VMEM((SB_QB, SB_GROUP * SB_TQ, SB_DH), F32),
                        pltpu.VMEM((SB_QB, SB_GROUP * SB_TQ, SB_TK), F32)],
        compiler_params=_cparams(("arbitrary", "arbitrary")),
        name="stick_breaking",
    )(q, kt, v, u)


def _row_tile(t, want):
    return want if t % want == 0 else t


def kernel(x, norm1_g, norm2_g, gla_w_in, gla_w_gate_up, gla_b_gate, gla_head_g, gla_w_out,
           kv_norm_g, kv_w, sb_w_q, sb_w_out, mlp_w_up, mlp_w_down, final_g):
    b, t, d = x.shape
    assert b == 1 and d == D_MODEL and t % GLA_CHUNK == 0
    x2 = x.reshape(t, d)
    tm_big = _row_tile(t, 1024)
    tn = 1024
    tm = _row_tile(t, 512)
    tf = 1024

    w_in = gla_w_in[0].astype(BF16)
    o_k = GLA_QK
    o_v = 2 * GLA_QK
    o_r = o_v + GLA_V
    o_a = o_r + GLA_V
    w_a = jnp.pad(w_in[:, o_a:], ((0, 0), (0, LANES - GLA_RANK)))
    w_g = jnp.pad(gla_w_gate_up[0], ((0, LANES - GLA_RANK), (0, 0)))

    kv_w3 = kv_w.reshape(1, *kv_w.shape)
    n_sq = int((d * D_FF) ** 0.5)
    up_sq = mlp_w_up.reshape(-1, n_sq, n_sq)
    down_sq = mlp_w_down.reshape(-1, n_sq, n_sq)

    q, xn, w_out0 = _norm_mm(x2, norm1_g[0], w_in, F32, tm=tm_big, cols=(0, GLA_QK),
                             casts=[(gla_w_out, 0)], name="gla_q")
    k, w_q1 = _mm(xn, w_in, F32, tm=tm_big, tn=tn, cols=(o_k, GLA_QK),
                  casts=[(sb_w_q, 0)], name="gla_k")
    vt, w_up0 = _mm(xn, w_in, BF16, tm=tm_big, tn=tn, cols=(o_v, GLA_V), transposed=True,
                    casts=[(mlp_w_up, 0)], name="gla_vt")
    r, w_down0 = _mm(xn, w_in, BF16, tm=tm_big, tn=tn, cols=(o_r, GLA_V),
                     casts=[(mlp_w_down, 0)], name="gla_r")
    a = _mm(xn, w_a, BF16, tm=tm_big, tn=LANES, name="gla_a")
    y = _gla(q, k, vt, r, a, w_g, gla_b_gate[0], gla_head_g[0])
    h, xn, w_out1, w_kv = _mm_res_norm(y, w_out0, x2, norm2_g[0], tm=tm,
                                       casts=[(sb_w_out, 0), (kv_w3, 0)], name="gla_out")
    h, xn_q, xn_kv, w_up1, w_down1 = _mlp(xn, h, w_up0, w_down0, norm1_g[1], kv_norm_g,
                                          tm=tm, tf=tf, final=False,
                                          casts=[(up_sq, 1), (down_sq, 1)], name="mlp0")
    w_up1 = w_up1.reshape(d, D_FF)
    w_down1 = w_down1.reshape(D_FF, d)

    q2 = _mm(xn_q, w_q1, BF16, tm=tm_big, tn=tn, scale=SB_DH ** -0.5, name="sb_q")
    kt = _mm(xn_kv, w_kv, BF16, tm=tm_big, tn=tn, cols=(0, SB_KV), transposed=True, name="shared_kt")
    v = _mm(xn_kv, w_kv, BF16, tm=tm_big, tn=tn, cols=(SB_KV, SB_KV), name="shared_v")
    o = _sb_attention(q2, kt, v)
    h, xn = _mm_res_norm(o, w_out1, h, norm2_g[1], tm=tm, name="sb_out")
    out, = _mlp(xn, h, w_up1, w_down1, final_g, final_g, tm=tm, tf=tf, final=True, name="mlp1")
    return out.reshape(b, t, d)
```

```python
import functools

import jax
import jax.numpy as jnp
from jax import lax
from jax.experimental import pallas as pl
from jax.experimental.pallas import tpu as pltpu

F32 = jnp.float32
BF16 = jnp.bfloat16

EPS = 1e-6
D_MODEL = 2048
D_FF = 4 * D_MODEL

GLA_HEADS = 4
GLA_DK = 256
GLA_DV = 512
GLA_QK = GLA_HEADS * GLA_DK
GLA_V = GLA_HEADS * GLA_DV
GLA_RANK = 16
GLA_TAU = 16.0
GLA_CHUNK = 128
GLA_DIAG = 8
GLA_LEVELS = (64, 32, 16, 8)
GLA_SAFE_SPAN = 60.0

SB_HEADS = 16
SB_DH = 128
SB_KVH = 4
SB_GROUP = SB_HEADS // SB_KVH
SB_KV = SB_KVH * SB_DH
SB_TQ = 128
SB_TK = 128
SB_FUSED = 3
SB_QB = 2
SB_EXIT = -110.0

LANES = 128
BF16_SUBLANES = 16
VMEM_LIMIT = 56 * 1024 * 1024

NT_DIMS = (((1,), (1,)), ((), ()))
TN_NT_DIMS = (((0,), (1,)), ((), ()))


def _cparams(sem):
    return pltpu.CompilerParams(dimension_semantics=sem, vmem_limit_bytes=VMEM_LIMIT)


def _rms(x, g):
    return x * lax.rsqrt(jnp.mean(x * x, axis=-1, keepdims=True) + EPS) * g


def _split(x):
    hi = x.astype(BF16)
    lo = (x - hi.astype(F32)).astype(BF16)
    return hi, lo


def _dot(a, b):
    return jnp.dot(a, b, preferred_element_type=F32)


def _dot_nt(a, b):
    return lax.dot_general(a, b, NT_DIMS, preferred_element_type=F32)


def _log1p_exp_neg_abs(z):
    return jnp.log(1.0 + jnp.exp(-jnp.abs(z)))


def _cast_along(weights, grid):
    steps = 1
    for g in grid:
        steps *= g

    def step_index(*ids):
        flat = ids[0]
        for g, i in zip(grid[1:], ids[1:]):
            flat = flat * g + i
        return flat

    operands, in_specs, out_specs, out_shapes = [], [], [], []
    for arr, layer in weights:
        _, rows, cols = arr.shape
        slab = rows // steps
        assert slab * steps == rows and slab % BF16_SUBLANES == 0
        operands.append(arr)
        in_specs.append(pl.BlockSpec((None, slab, cols),
                                     lambda *ids, layer=layer: (layer, step_index(*ids), 0)))
        out_specs.append(pl.BlockSpec((slab, cols), lambda *ids: (step_index(*ids), 0)))
        out_shapes.append(jax.ShapeDtypeStruct((rows, cols), BF16))
    return operands, in_specs, out_specs, out_shapes


def _store_casts(src_refs, dst_refs):
    for src, dst in zip(src_refs, dst_refs):
        dst[...] = src[...].astype(dst.dtype)


def _mm_kernel(x_ref, w_ref, *refs, nt, scale, ncast):
    o_ref = refs[ncast]
    if nt:
        acc = lax.dot_general(w_ref[...], x_ref[...], TN_NT_DIMS,
                              preferred_element_type=F32)
    else:
        acc = _dot(x_ref[...], w_ref[...])
    if scale != 1.0:
        acc = acc * scale
    o_ref[...] = acc.astype(o_ref.dtype)
    _store_casts(refs[:ncast], refs[ncast + 1:])


def _mm_call(x, w, out_dtype, *, tm, tn, nt, j0, n, scale, casts, name):
    t, k = x.shape
    grid = (t // tm, n // tn)
    c_ops, c_in, c_out, c_shapes = _cast_along(casts, grid)
    w_spec = pl.BlockSpec((k, tn), lambda i, j: (0, j0 + j))
    if nt:
        o_spec = pl.BlockSpec((tn, tm), lambda i, j: (j, i))
        o_shape = jax.ShapeDtypeStruct((n, t), out_dtype)
    else:
        o_spec = pl.BlockSpec((tm, tn), lambda i, j: (i, j))
        o_shape = jax.ShapeDtypeStruct((t, n), out_dtype)
    outs = pl.pallas_call(
        functools.partial(_mm_kernel, nt=nt, scale=scale, ncast=len(casts)),
        grid=grid,
        in_specs=[pl.BlockSpec((tm, k), lambda i, j: (i, 0)), w_spec] + c_in,
        out_specs=[o_spec] + c_out,
        out_shape=[o_shape] + c_shapes,
        compiler_params=_cparams(("parallel", "arbitrary")),
        name=name,
    )(x, w, *c_ops)
    return outs if casts else outs[0]


def _mm(x, w, out_dtype, *, tm, tn, scale=1.0, cols=None, casts=(), transposed=False, name="proj"):
    c0, n = cols if cols is not None else (0, w.shape[1])
    tn = min(tn, n)
    assert c0 % tn == 0 and n % tn == 0
    return _mm_call(x, w, out_dtype, tm=tm, tn=tn, nt=transposed, j0=c0 // tn, n=n, scale=scale,
                    casts=casts, name=name)


def _norm_mm_kernel(x_ref, g_ref, w_ref, *refs, ncast):
    o_ref, xn_ref = refs[ncast:ncast + 2]
    xn = _rms(x_ref[...], g_ref[...]).astype(xn_ref.dtype)
    xn_ref[...] = xn
    o_ref[...] = _dot(xn, w_ref[...].astype(BF16)).astype(o_ref.dtype)
    _store_casts(refs[:ncast], refs[ncast + 2:])


def _norm_mm(x, g, w, out_dtype, *, tm, cols, casts=(), name):
    t, k = x.shape
    c0, n = cols
    assert c0 % n == 0
    grid = (t // tm,)
    c_ops, c_in, c_out, c_shapes = _cast_along(casts, grid)
    return pl.pallas_call(
        functools.partial(_norm_mm_kernel, ncast=len(casts)),
        grid=grid,
        in_specs=[pl.BlockSpec((tm, k), lambda i: (i, 0)),
                  pl.BlockSpec((1, k), lambda i: (0, 0)),
                  pl.BlockSpec((None, k, n), lambda i: (0, 0, c0 // n))] + c_in,
        out_specs=[pl.BlockSpec((tm, n), lambda i: (i, 0)),
                   pl.BlockSpec((tm, k), lambda i: (i, 0))] + c_out,
        out_shape=[jax.ShapeDtypeStruct((t, n), out_dtype),
                   jax.ShapeDtypeStruct((t, k), BF16)] + c_shapes,
        compiler_params=_cparams(("parallel",)),
        name=name,
    )(x, g.reshape(1, k), w, *c_ops)


def _mm_res_norm_kernel(y_ref, w_ref, res_ref, g_ref, *refs, ncast):
    h_ref, xn_ref = refs[ncast:ncast + 2]
    h = res_ref[...] + _dot(y_ref[...], w_ref[...])
    h_ref[...] = h
    xn_ref[...] = _rms(h, g_ref[...]).astype(xn_ref.dtype)
    _store_casts(refs[:ncast], refs[ncast + 2:])


def _mm_res_norm(y, w, res, g, *, tm, casts=(), name):
    t, k = y.shape
    n = w.shape[1]
    grid = (t // tm,)
    c_ops, c_in, c_out, c_shapes = _cast_along(casts, grid)
    return pl.pallas_call(
        functools.partial(_mm_res_norm_kernel, ncast=len(casts)),
        grid=grid,
        in_specs=[pl.BlockSpec((tm, k), lambda i: (i, 0)),
                  pl.BlockSpec((k, n), lambda i: (0, 0)),
                  pl.BlockSpec((tm, n), lambda i: (i, 0)),
                  pl.BlockSpec((1, n), lambda i: (0, 0))] + c_in,
        out_specs=[pl.BlockSpec((tm, n), lambda i: (i, 0)),
                   pl.BlockSpec((tm, n), lambda i: (i, 0))] + c_out,
        out_shape=[jax.ShapeDtypeStruct((t, n), F32),
                   jax.ShapeDtypeStruct((t, n), BF16)] + c_shapes,
        compiler_params=_cparams(("parallel",)),
        name=name,
    )(y, w, res, g.reshape(1, n), *c_ops)


def _mlp_kernel(xn_ref, h_ref, wu_ref, wd_ref, g1_ref, g2_ref, *refs, final, ncast):
    n_out = 1 if final else 3
    acc_ref = refs[ncast]
    norm_refs = refs[ncast + 1:ncast + n_out]
    _store_casts(refs[:ncast], refs[ncast + n_out:])
    f = pl.program_id(1)

    @pl.when(f == 0)
    def _():
        acc_ref[...] = h_ref[...]

    u = _dot(xn_ref[...], wu_ref[...])
    a = jnp.square(jnp.maximum(u, 0.0)).astype(BF16)
    acc_ref[...] += _dot(a, wd_ref[...])

    @pl.when(f == pl.num_programs(1) - 1)
    def _():
        h = acc_ref[...]
        hn = h * lax.rsqrt(jnp.mean(h * h, axis=-1, keepdims=True) + EPS)
        if final:
            acc_ref[...] = hn * g1_ref[...]
        else:
            xa_ref, xb_ref = norm_refs
            xa_ref[...] = (hn * g1_ref[...]).astype(BF16)
            xb_ref[...] = (hn * g2_ref[...]).astype(BF16)


def _mlp(xn, h, wu, wd, g1, g2, *, tm, tf, final, casts=(), name):
    t, d = xn.shape
    ff = wu.shape[1]
    grid = (t // tm, ff // tf)
    c_ops, c_in, c_out, c_shapes = _cast_along(casts, grid)
    row = pl.BlockSpec((tm, d), lambda i, f: (i, 0))
    vec = pl.BlockSpec((1, d), lambda i, f: (0, 0))
    out_specs = [row]
    out_shape = [jax.ShapeDtypeStruct((t, d), F32)]
    if not final:
        out_specs += [row, row]
        out_shape += [jax.ShapeDtypeStruct((t, d), BF16), jax.ShapeDtypeStruct((t, d), BF16)]
    return pl.pallas_call(
        functools.partial(_mlp_kernel, final=final, ncast=len(casts)),
        grid=grid,
        in_specs=[row, row,
                  pl.BlockSpec((d, tf), lambda i, f: (0, f)),
                  pl.BlockSpec((tf, d), lambda i, f: (f, 0)),
                  vec, vec] + c_in,
        out_specs=out_specs + c_out,
        out_shape=out_shape + c_shapes,
        compiler_params=_cparams(("parallel", "arbitrary")),
        name=name,
    )(xn, h, wu, wd, g1.reshape(1, d), g2.reshape(1, d), *c_ops)


def _gla_kernel(q_ref, k_ref, vt_ref, r_ref, a_ref, a_next_ref, wg_ref, bg_ref, gh_ref,
                y_ref, st_ref, cum2_ref, mild_ref):
    c = GLA_CHUNK
    nd = GLA_DIAG
    step = pl.program_id(0)
    slot = step % 2
    cum_ref = cum2_ref.at[slot]
    ri = lax.broadcasted_iota(jnp.int32, (c, c), 0)
    ci = lax.broadcasted_iota(jnp.int32, (c, c), 1)

    def gate(a, dst):
        z = _dot(a, wg_ref[...]) + bg_ref[...]
        log_alpha = (jnp.minimum(z, 0.0) - _log1p_exp_neg_abs(z)) * (1.0 / GLA_TAU)
        tri = (ci <= ri).astype(BF16)
        la_hi, la_lo = _split(log_alpha)
        cum = _dot(tri, la_hi) + _dot(tri, la_lo)
        cum2_ref[dst] = cum
        mild_ref[dst] = (jnp.max(-cum[c - 1:c, :]) <= GLA_SAFE_SPAN).astype(jnp.int32)

    @pl.when(step == 0)
    def _():
        st_ref[...] = jnp.zeros_like(st_ref)
        gate(a_ref[...], 0)

    row_k = lax.broadcasted_iota(jnp.int32, (c, GLA_DK), 0)
    row_d = lax.broadcasted_iota(jnp.int32, (nd, GLA_DK), 0)
    lane_d = lax.broadcasted_iota(jnp.int32, (nd, c), 1)

    def scores_one_product(ks, ch, kh, qs, qe):
        k_inv = (kh * jnp.exp(-ch)).astype(BF16)
        return jnp.where(ci <= ri, _dot_nt(qe, k_inv), 0.0)

    def scores_any_decay(ks, ch, kh, qs, qe):
        p = jnp.zeros((c, c), F32)
        for b in GLA_LEVELS:
            upper = (row_k % (2 * b)) >= b
            same_group = (ri // (2 * b)) == (ci // (2 * b))
            ref = jnp.concatenate(
                [jnp.broadcast_to(cum_ref[g + b:g + b + 1, ks], (2 * b, GLA_DK))
                 for g in range(0, c, 2 * b)], axis=0)
            e = jnp.exp(-jnp.abs(ch - ref))
            q_sc = jnp.where(upper, qs * e, 0.0).astype(BF16)
            k_sc = jnp.where(upper, 0.0, kh * e).astype(BF16)
            p = p + jnp.where(same_group, _dot_nt(q_sc, k_sc), 0.0)
        p_rows = []
        for r0 in range(0, c, nd):
            c_i = ch[r0:r0 + nd]
            q_i = qs[r0:r0 + nd]
            p_i = p[r0:r0 + nd]
            for j in range(nd):
                c_j = cum_ref[r0 + j:r0 + j + 1, ks]
                k_j = k_ref[r0 + j:r0 + j + 1, ks]
                dec = jnp.exp(jnp.where(row_d >= j, c_i - c_j, -jnp.inf))
                col = jnp.sum(q_i * k_j * dec, axis=1, keepdims=True)
                p_i = jnp.where(lane_d == r0 + j, col, p_i)
            p_rows.append(p_i)
        return jnp.concatenate(p_rows, axis=0)

    def all_heads(scores):
        for h in range(GLA_HEADS):
            ks = slice(h * GLA_DK, (h + 1) * GLA_DK)
            vs = slice(h * GLA_DV, (h + 1) * GLA_DV)
            ch = cum_ref[:, ks]
            kh = k_ref[:, ks]
            qs = q_ref[:, ks] * (GLA_DK ** -0.5)
            vt = vt_ref[vs, :]
            st = st_ref[h]

            qe = (qs * jnp.exp(ch)).astype(BF16)
            o = _dot_nt(qe, st.astype(BF16))
            p = scores(ks, ch, kh, qs, qe)
            o = o + _dot_nt(p.astype(BF16), vt)

            last = cum_ref[c - 1:c, ks]
            kd = (kh * jnp.exp(last - ch)).astype(BF16)
            st_ref[h] = st * jnp.exp(last) + _dot(vt, kd)

            on = _rms(o, gh_ref[...])
            rr = r_ref[:, vs].astype(F32)
            y_ref[:, vs] = (on * (rr * jax.nn.sigmoid(rr))).astype(y_ref.dtype)
        gate(a_next_ref[...], 1 - slot)

    mild = mild_ref[slot] != 0

    @pl.when(mild)
    def _():
        all_heads(scores_one_product)

    @pl.when(jnp.logical_not(mild))
    def _():
        all_heads(scores_any_decay)


def _gla(q, k, vt, r, a, wg, bg, gh):
    t = q.shape[0]
    c = GLA_CHUNK
    const = lambda shape: pl.BlockSpec(shape, lambda i: (0, 0))
    last = t // c - 1
    return pl.pallas_call(
        _gla_kernel,
        grid=(t // c,),
        in_specs=[pl.BlockSpec((c, GLA_QK), lambda i: (i, 0)),
                  pl.BlockSpec((c, GLA_QK), lambda i: (i, 0)),
                  pl.BlockSpec((GLA_V, c), lambda i: (0, i)),
                  pl.BlockSpec((c, GLA_V), lambda i: (i, 0)),
                  pl.BlockSpec((c, LANES), lambda i: (i, 0)),
                  pl.BlockSpec((c, LANES), lambda i: (jnp.minimum(i + 1, last), 0)),
                  const((LANES, GLA_QK)),
                  const((1, GLA_QK)),
                  const((1, GLA_DV))],
        out_specs=pl.BlockSpec((c, GLA_V), lambda i: (i, 0)),
        out_shape=jax.ShapeDtypeStruct((t, GLA_V), BF16),
        scratch_shapes=[pltpu.VMEM((GLA_HEADS, GLA_DV, GLA_DK), F32),
                        pltpu.VMEM((2, c, GLA_QK), F32),
                        pltpu.SMEM((2,), jnp.int32)],
        compiler_params=_cparams(("arbitrary",)),
        name="gla_scan",
    )(q, k, vt, r, a, a, wg.astype(BF16), bg.reshape(1, GLA_QK), gh.reshape(1, GLA_DV))


def _sb_sum_matrix():
    tk = SB_TK
    n = SB_FUSED * tk
    j = jnp.arange(n)[:, None]
    s = jnp.arange(n)[None, :]
    later = (j > s)
    cols = [later[:, :tk], jnp.ones((n, tk), bool)] + [later[:, i * tk:(i + 1) * tk] for i in range(1, SB_FUSED)]
    return -jnp.concatenate(cols, axis=1).astype(BF16)


def _sb_kernel(q_ref, kt_ref, v_ref, u_ref, o_ref, acc_ref, carry_ref):
    tq, tk = SB_TQ, SB_TK
    rows = SB_GROUP * tq
    nf = SB_FUSED
    step = pl.program_id(1)
    qpos = lax.broadcasted_iota(jnp.int32, (rows, tk), 0) % tq
    kpos = lax.broadcasted_iota(jnp.int32, (rows, tk), 1)
    causal = kpos < qpos

    blocks = []
    for s in range(SB_QB):
        qs = jnp.concatenate([q_ref[s * tq:(s + 1) * tq, g * SB_DH:(g + 1) * SB_DH]
                              for g in range(SB_GROUP)], axis=0)
        blocks.append((s, step * SB_QB + s, qs))

    def tile(s, qs, kt, masked):
        k0 = pl.multiple_of(kt * tk, tk)
        z = _dot(qs, kt_ref[:, pl.ds(k0, tk)])
        sp = jnp.maximum(z, 0.0) + _log1p_exp_neg_abs(z)
        spm = jnp.where(causal, sp, 0.0) if masked else sp
        spb = spm.astype(BF16)
        tot = _dot(spb, u_ref[(nf - 1) * tk:, tk:2 * tk])
        tail = _dot(spb, u_ref[(nf - 1) * tk:, nf * tk:]) + carry_ref[s]
        w = jnp.exp((z - sp) + tail)
        if masked:
            w = jnp.where(causal, w, 0.0)
        acc_ref[s] += _dot(w.astype(BF16), v_ref[pl.ds(k0, tk), :])
        carry = carry_ref[s] + tot
        carry_ref[s] = carry
        return jnp.max(carry)

    def first_fused(s, qi, qs):
        k0 = pl.multiple_of((qi - (nf - 1)) * tk, tk)
        z = _dot(qs, kt_ref[:, pl.ds(k0, nf * tk)])
        sp = jnp.maximum(z, 0.0) + _log1p_exp_neg_abs(z)
        spm = jnp.concatenate([sp[:, :(nf - 1) * tk],
                               jnp.where(causal, sp[:, (nf - 1) * tk:], 0.0)], axis=1)
        spb = spm.astype(BF16)
        first = _dot(spb, u_ref[:, :2 * tk])
        later = _dot(spb[:, tk:], u_ref[tk:, 2 * tk:])
        tot = first[:, tk:]
        w = jnp.exp((z - sp) + jnp.concatenate([first[:, :tk], later], axis=1))
        w = jnp.concatenate([w[:, :(nf - 1) * tk],
                             jnp.where(causal, w[:, (nf - 1) * tk:], 0.0)], axis=1)
        acc_ref[s] = _dot(w.astype(BF16), v_ref[pl.ds(k0, nf * tk), :])
        carry_ref[s] = tot
        return qi - nf, jnp.max(tot)

    def first_single(s, qi, qs):
        acc_ref[s] = jnp.zeros((rows, SB_DH), F32)
        carry_ref[s] = jnp.zeros((rows, tk), F32)
        return qi - 1, tile(s, qs, qi, True)

    def all_fused():
        return tuple(first_fused(*blk) for blk in blocks)

    def all_single():
        return tuple(first_single(*blk) for blk in blocks)

    starts = lax.cond(step * SB_QB >= nf - 1, all_fused, all_single)

    def unfinished(state):
        kt, mx = state
        return jnp.logical_and(kt >= 0, mx > SB_EXIT)

    for (s, _, qs), start in zip(blocks, starts):
        def earlier_tile(state, s=s, qs=qs):
            kt, _ = state
            return kt - 1, tile(s, qs, kt, False)

        lax.while_loop(unfinished, earlier_tile, start)
        for g in range(SB_GROUP):
            o_ref[s * tq:(s + 1) * tq, g * SB_DH:(g + 1) * SB_DH] = (
                acc_ref[s, g * tq:(g + 1) * tq, :].astype(o_ref.dtype))


def _sb_attention(q, kt, v):
    t = q.shape[0]
    tq = SB_TQ * SB_QB
    gw = SB_GROUP * SB_DH
    u = _sb_sum_matrix()
    return pl.pallas_call(
        _sb_kernel,
        grid=(SB_KVH, t // tq),
        in_specs=[pl.BlockSpec((tq, gw), lambda n, i: (i, n)),
                  pl.BlockSpec((SB_DH, t), lambda n, i: (n, 0)),
                  pl.BlockSpec((t, SB_DH), lambda n, i: (0, n)),
                  pl.BlockSpec(u.shape, lambda n, i: (0, 0))],
        out_specs=pl.BlockSpec((tq, gw), lambda n, i: (i, n)),
        out_shape=jax.ShapeDtypeStruct((t, SB_HEADS * SB_DH), BF16),
        scratch_shapes=[pltpu.VMEM((SB_QB, SB_GROUP * SB_TQ, SB_DH), F32),
                        pltpu.VMEM((SB_QB, SB_GROUP * SB_TQ, SB_TK), F32)],
        compiler_params=_cparams(("arbitrary", "arbitrary")),
        name="stick_breaking",
    )(q, kt, v, u)


def _row_tile(t, want):
    return want if t % want == 0 else t


def kernel(x, norm1_g, norm2_g, gla_w_in, gla_w_gate_up, gla_b_gate, gla_head_g, gla_w_out,
           kv_norm_g, kv_w, sb_w_q, sb_w_out, mlp_w_up, mlp_w_down, final_g):
    b, t, d = x.shape
    assert b == 1 and d == D_MODEL and t % GLA_CHUNK == 0
    x2 = x.reshape(t, d)
    tm_big = _row_tile(t, 1024)
    tn = 1024
    tm = _row_tile(t, 512)
    tf = 1024

    o_k = GLA_QK
    o_v = 2 * GLA_QK
    o_r = o_v + GLA_V
    o_a = o_r + GLA_V
    w_g = jnp.pad(gla_w_gate_up[0], ((0, LANES - GLA_RANK), (0, 0)))

    kv_w3 = kv_w.reshape(1, *kv_w.shape)

    q, xn, w_out0, w_in = _norm_mm(x2, norm1_g[0], gla_w_in, F32, tm=tm, cols=(0, GLA_QK),
                                   casts=[(gla_w_out, 0), (gla_w_in, 0)], name="gla_q")
    w_a = jnp.pad(w_in[:, o_a:], ((0, 0), (0, LANES - GLA_RANK)))
    k, w_q1 = _mm(xn, w_in, F32, tm=tm_big, tn=tn, cols=(o_k, GLA_QK),
                  casts=[(sb_w_q, 0)], name="gla_k")
    vt, w_up0 = _mm(xn, w_in, BF16, tm=tm_big, tn=tn, cols=(o_v, GLA_V), transposed=True,
                    casts=[(mlp_w_up, 0)], name="gla_vt")
    r, w_down0 = _mm(xn, w_in, BF16, tm=tm_big, tn=tn, cols=(o_r, GLA_V),
                     casts=[(mlp_w_down, 0)], name="gla_r")
    a = _mm(xn, w_a, BF16, tm=tm_big, tn=LANES, name="gla_a")
    y = _gla(q, k, vt, r, a, w_g, gla_b_gate[0], gla_head_g[0])
    h, xn, w_out1, w_kv = _mm_res_norm(y, w_out0, x2, norm2_g[0], tm=tm,
                                       casts=[(sb_w_out, 0), (kv_w3, 0)], name="gla_out")
    h, xn_q, xn_kv = _mlp(xn, h, w_up0, w_down0, norm1_g[1], kv_norm_g,
                          tm=tm, tf=tf, final=False, name="mlp0")

    q2, w_up1 = _mm(xn_q, w_q1, BF16, tm=tm_big, tn=tn, scale=SB_DH ** -0.5,
                    casts=[(mlp_w_up, 1)], name="sb_q")
    kt = _mm(xn_kv, w_kv, BF16, tm=tm_big, tn=tn, cols=(0, SB_KV), transposed=True, name="shared_kt")
    v = _mm(xn_kv, w_kv, BF16, tm=tm_big, tn=tn, cols=(SB_KV, SB_KV), name="shared_v")
    o = _sb_attention(q2, kt, v)
    h, xn, w_down1 = _mm_res_norm(o, w_out1, h, norm2_g[1], tm=tm,
                                  casts=[(mlp_w_down, 1)], name="sb_out")
    out, = _mlp(xn, h, w_up1, w_down1, final_g, final_g, tm=tm, tf=tf, final=True, name="mlp1")
    return out.reshape(b, t, d)
```

```python
import functools

import jax
import jax.numpy as jnp
from jax import lax
from jax.experimental import pallas as pl
from jax.experimental.pallas import tpu as pltpu

F32 = jnp.float32
BF16 = jnp.bfloat16

EPS = 1e-6
D_MODEL = 2048
D_FF = 4 * D_MODEL

GLA_HEADS = 4
GLA_DK = 256
GLA_DV = 512
GLA_QK = GLA_HEADS * GLA_DK
GLA_V = GLA_HEADS * GLA_DV
GLA_RANK = 16
GLA_TAU = 16.0
GLA_CHUNK = 128
GLA_DIAG = 8
GLA_LEVELS = (64, 32, 16, 8)
GLA_SAFE_SPAN = 60.0

SB_HEADS = 16
SB_DH = 128
SB_KVH = 4
SB_GROUP = SB_HEADS // SB_KVH
SB_KV = SB_KVH * SB_DH
SB_TQ = 128
SB_TK = 128
SB_FUSED = 3
SB_QB = 2
SB_EXIT = -110.0

LANES = 128
BF16_SUBLANES = 16
VMEM_LIMIT = 56 * 1024 * 1024

NT_DIMS = (((1,), (1,)), ((), ()))
TN_NT_DIMS = (((0,), (1,)), ((), ()))


def _cparams(sem):
    return pltpu.CompilerParams(dimension_semantics=sem, vmem_limit_bytes=VMEM_LIMIT)


def _rms(x, g):
    return x * lax.rsqrt(jnp.mean(x * x, axis=-1, keepdims=True) + EPS) * g


def _split(x):
    hi = x.astype(BF16)
    lo = (x - hi.astype(F32)).astype(BF16)
    return hi, lo


def _dot(a, b):
    return jnp.dot(a, b, preferred_element_type=F32)


def _dot_nt(a, b):
    return lax.dot_general(a, b, NT_DIMS, preferred_element_type=F32)


def _log1p_exp_neg_abs(z):
    return jnp.log(1.0 + jnp.exp(-jnp.abs(z)))


def _cast_along(weights, grid):
    steps = 1
    for g in grid:
        steps *= g

    def step_index(*ids):
        flat = ids[0]
        for g, i in zip(grid[1:], ids[1:]):
            flat = flat * g + i
        return flat

    operands, in_specs, out_specs, out_shapes = [], [], [], []
    for arr, layer in weights:
        _, rows, cols = arr.shape
        slab = rows // steps
        assert slab * steps == rows and slab % BF16_SUBLANES == 0
        operands.append(arr)
        in_specs.append(pl.BlockSpec((None, slab, cols),
                                     lambda *ids, layer=layer: (layer, step_index(*ids), 0)))
        out_specs.append(pl.BlockSpec((slab, cols), lambda *ids: (step_index(*ids), 0)))
        out_shapes.append(jax.ShapeDtypeStruct((rows, cols), BF16))
    return operands, in_specs, out_specs, out_shapes


def _store_casts(src_refs, dst_refs):
    for src, dst in zip(src_refs, dst_refs):
        dst[...] = src[...].astype(dst.dtype)


def _mm_kernel(x_ref, w_ref, *refs, nt, scale, ncast):
    o_ref = refs[ncast]
    if nt:
        acc = lax.dot_general(w_ref[...], x_ref[...], TN_NT_DIMS,
                              preferred_element_type=F32)
    else:
        acc = _dot(x_ref[...], w_ref[...])
    if scale != 1.0:
        acc = acc * scale
    o_ref[...] = acc.astype(o_ref.dtype)
    _store_casts(refs[:ncast], refs[ncast + 1:])


def _mm_call(x, w, out_dtype, *, tm, tn, nt, j0, n, scale, casts, name):
    t, k = x.shape
    grid = (t // tm, n // tn)
    c_ops, c_in, c_out, c_shapes = _cast_along(casts, grid)
    w_spec = pl.BlockSpec((k, tn), lambda i, j: (0, j0 + j))
    if nt:
        o_spec = pl.BlockSpec((tn, tm), lambda i, j: (j, i))
        o_shape = jax.ShapeDtypeStruct((n, t), out_dtype)
    else:
        o_spec = pl.BlockSpec((tm, tn), lambda i, j: (i, j))
        o_shape = jax.ShapeDtypeStruct((t, n), out_dtype)
    outs = pl.pallas_call(
        functools.partial(_mm_kernel, nt=nt, scale=scale, ncast=len(casts)),
        grid=grid,
        in_specs=[pl.BlockSpec((tm, k), lambda i, j: (i, 0)), w_spec] + c_in,
        out_specs=[o_spec] + c_out,
        out_shape=[o_shape] + c_shapes,
        compiler_params=_cparams(("parallel", "arbitrary")),
        name=name,
    )(x, w, *c_ops)
    return outs if casts else outs[0]


def _mm(x, w, out_dtype, *, tm, tn, scale=1.0, cols=None, casts=(), transposed=False, name="proj"):
    c0, n = cols if cols is not None else (0, w.shape[1])
    tn = min(tn, n)
    assert c0 % tn == 0 and n % tn == 0
    return _mm_call(x, w, out_dtype, tm=tm, tn=tn, nt=transposed, j0=c0 // tn, n=n, scale=scale,
                    casts=casts, name=name)


def _norm_mm_kernel(x_ref, g_ref, w_ref, *refs, ncast):
    o_ref, xn_ref = refs[ncast:ncast + 2]
    xn = _rms(x_ref[...], g_ref[...]).astype(xn_ref.dtype)
    xn_ref[...] = xn
    o_ref[...] = _dot(xn, w_ref[...]).astype(o_ref.dtype)
    _store_casts(refs[:ncast], refs[ncast + 2:])


def _norm_mm(x, g, w, out_dtype, *, tm, cols, casts=(), name):
    t, k = x.shape
    c0, n = cols
    assert c0 % n == 0
    grid = (t // tm,)
    c_ops, c_in, c_out, c_shapes = _cast_along(casts, grid)
    return pl.pallas_call(
        functools.partial(_norm_mm_kernel, ncast=len(casts)),
        grid=grid,
        in_specs=[pl.BlockSpec((tm, k), lambda i: (i, 0)),
                  pl.BlockSpec((1, k), lambda i: (0, 0)),
                  pl.BlockSpec((k, n), lambda i: (0, c0 // n))] + c_in,
        out_specs=[pl.BlockSpec((tm, n), lambda i: (i, 0)),
                   pl.BlockSpec((tm, k), lambda i: (i, 0))] + c_out,
        out_shape=[jax.ShapeDtypeStruct((t, n), out_dtype),
                   jax.ShapeDtypeStruct((t, k), BF16)] + c_shapes,
        compiler_params=_cparams(("parallel",)),
        name=name,
    )(x, g.reshape(1, k), w, *c_ops)


def _mm_res_norm_kernel(y_ref, w_ref, res_ref, g_ref, *refs, ncast):
    h_ref, xn_ref = refs[ncast:ncast + 2]
    h = res_ref[...] + _dot(y_ref[...], w_ref[...])
    h_ref[...] = h
    xn_ref[...] = _rms(h, g_ref[...]).astype(xn_ref.dtype)
    _store_casts(refs[:ncast], refs[ncast + 2:])


def _mm_res_norm(y, w, res, g, *, tm, casts=(), name):
    t, k = y.shape
    n = w.shape[1]
    grid = (t // tm,)
    c_ops, c_in, c_out, c_shapes = _cast_along(casts, grid)
    return pl.pallas_call(
        functools.partial(_mm_res_norm_kernel, ncast=len(casts)),
        grid=grid,
        in_specs=[pl.BlockSpec((tm, k), lambda i: (i, 0)),
                  pl.BlockSpec((k, n), lambda i: (0, 0)),
                  pl.BlockSpec((tm, n), lambda i: (i, 0)),
                  pl.BlockSpec((1, n), lambda i: (0, 0))] + c_in,
        out_specs=[pl.BlockSpec((tm, n), lambda i: (i, 0)),
                   pl.BlockSpec((tm, n), lambda i: (i, 0))] + c_out,
        out_shape=[jax.ShapeDtypeStruct((t, n), F32),
                   jax.ShapeDtypeStruct((t, n), BF16)] + c_shapes,
        compiler_params=_cparams(("parallel",)),
        name=name,
    )(y, w, res, g.reshape(1, n), *c_ops)


def _mlp_kernel(xn_ref, h_ref, wu_ref, wd_ref, g1_ref, g2_ref, *refs, final, ncast):
    n_out = 1 if final else 3
    acc_ref = refs[ncast]
    norm_refs = refs[ncast + 1:ncast + n_out]
    _store_casts(refs[:ncast], refs[ncast + n_out:])
    f = pl.program_id(1)

    @pl.when(f == 0)
    def _():
        acc_ref[...] = h_ref[...]

    u = _dot(xn_ref[...], wu_ref[...])
    a = jnp.square(jnp.maximum(u, 0.0)).astype(BF16)
    acc_ref[...] += _dot(a, wd_ref[...])

    @pl.when(f == pl.num_programs(1) - 1)
    def _():
        h = acc_ref[...]
        hn = h * lax.rsqrt(jnp.mean(h * h, axis=-1, keepdims=True) + EPS)
        if final:
            acc_ref[...] = hn * g1_ref[...]
        else:
            xa_ref, xb_ref = norm_refs
            xa_ref[...] = (hn * g1_ref[...]).astype(BF16)
            xb_ref[...] = (hn * g2_ref[...]).astype(BF16)


def _mlp(xn, h, wu, wd, g1, g2, *, tm, tf, final, casts=(), name):
    t, d = xn.shape
    ff = wu.shape[1]
    grid = (t // tm, ff // tf)
    c_ops, c_in, c_out, c_shapes = _cast_along(casts, grid)
    row = pl.BlockSpec((tm, d), lambda i, f: (i, 0))
    vec = pl.BlockSpec((1, d), lambda i, f: (0, 0))
    out_specs = [row]
    out_shape = [jax.ShapeDtypeStruct((t, d), F32)]
    if not final:
        out_specs += [row, row]
        out_shape += [jax.ShapeDtypeStruct((t, d), BF16), jax.ShapeDtypeStruct((t, d), BF16)]
    return pl.pallas_call(
        functools.partial(_mlp_kernel, final=final, ncast=len(casts)),
        grid=grid,
        in_specs=[row, row,
                  pl.BlockSpec((d, tf), lambda i, f: (0, f)),
                  pl.BlockSpec((tf, d), lambda i, f: (f, 0)),
                  vec, vec] + c_in,
        out_specs=out_specs + c_out,
        out_shape=out_shape + c_shapes,
        compiler_params=_cparams(("parallel", "arbitrary")),
        name=name,
    )(xn, h, wu, wd, g1.reshape(1, d), g2.reshape(1, d), *c_ops)


def _gla_kernel(q_ref, k_ref, vt_ref, r_ref, a_ref, a_next_ref, wg_ref, bg_ref, gh_ref,
                y_ref, st_ref, cum2_ref, mild_ref):
    c = GLA_CHUNK
    nd = GLA_DIAG
    step = pl.program_id(0)
    slot = step % 2
    cum_ref = cum2_ref.at[slot]
    ri = lax.broadcasted_iota(jnp.int32, (c, c), 0)
    ci = lax.broadcasted_iota(jnp.int32, (c, c), 1)

    def gate(a, dst):
        z = _dot(a, wg_ref[...]) + bg_ref[...]
        log_alpha = (jnp.minimum(z, 0.0) - _log1p_exp_neg_abs(z)) * (1.0 / GLA_TAU)
        tri = (ci <= ri).astype(BF16)
        la_hi, la_lo = _split(log_alpha)
        cum = _dot(tri, la_hi) + _dot(tri, la_lo)
        cum2_ref[dst] = cum
        mild_ref[dst] = (jnp.max(-cum[c - 1:c, :]) <= GLA_SAFE_SPAN).astype(jnp.int32)

    @pl.when(step == 0)
    def _():
        st_ref[...] = jnp.zeros_like(st_ref)
        gate(a_ref[...], 0)

    row_k = lax.broadcasted_iota(jnp.int32, (c, GLA_DK), 0)
    row_d = lax.broadcasted_iota(jnp.int32, (nd, GLA_DK), 0)
    lane_d = lax.broadcasted_iota(jnp.int32, (nd, c), 1)

    def scores_one_product(ks, ch, kh, qs, qe):
        k_inv = (kh * jnp.exp(-ch)).astype(BF16)
        return jnp.where(ci <= ri, _dot_nt(qe, k_inv), 0.0)

    def scores_any_decay(ks, ch, kh, qs, qe):
        p = jnp.zeros((c, c), F32)
        for b in GLA_LEVELS:
            upper = (row_k % (2 * b)) >= b
            same_group = (ri // (2 * b)) == (ci // (2 * b))
            ref = jnp.concatenate(
                [jnp.broadcast_to(cum_ref[g + b:g + b + 1, ks], (2 * b, GLA_DK))
                 for g in range(0, c, 2 * b)], axis=0)
            e = jnp.exp(-jnp.abs(ch - ref))
            q_sc = jnp.where(upper, qs * e, 0.0).astype(BF16)
            k_sc = jnp.where(upper, 0.0, kh * e).astype(BF16)
            p = p + jnp.where(same_group, _dot_nt(q_sc, k_sc), 0.0)
        p_rows = []
        for r0 in range(0, c, nd):
            c_i = ch[r0:r0 + nd]
            q_i = qs[r0:r0 + nd]
            p_i = p[r0:r0 + nd]
            for j in range(nd):
                c_j = cum_ref[r0 + j:r0 + j + 1, ks]
                k_j = k_ref[r0 + j:r0 + j + 1, ks]
                dec = jnp.exp(jnp.where(row_d >= j, c_i - c_j, -jnp.inf))
                col = jnp.sum(q_i * k_j * dec, axis=1, keepdims=True)
                p_i = jnp.where(lane_d == r0 + j, col, p_i)
            p_rows.append(p_i)
        return jnp.concatenate(p_rows, axis=0)

    def all_heads(scores):
        for h in range(GLA_HEADS):
            ks = slice(h * GLA_DK, (h + 1) * GLA_DK)
            vs = slice(h * GLA_DV, (h + 1) * GLA_DV)
            ch = cum_ref[:, ks]
            kh = k_ref[:, ks]
            qs = q_ref[:, ks] * (GLA_DK ** -0.5)
            vt = vt_ref[vs, :]
            st = st_ref[h]

            qe = (qs * jnp.exp(ch)).astype(BF16)
            o = _dot_nt(qe, st.astype(BF16))
            p = scores(ks, ch, kh, qs, qe)
            o = o + _dot_nt(p.astype(BF16), vt)

            last = cum_ref[c - 1:c, ks]
            kd = (kh * jnp.exp(last - ch)).astype(BF16)
            st_ref[h] = st * jnp.exp(last) + _dot(vt, kd)

            on = _rms(o, gh_ref[...])
            rr = r_ref[:, vs].astype(F32)
            y_ref[:, vs] = (on * (rr * jax.nn.sigmoid(rr))).astype(y_ref.dtype)
        gate(a_next_ref[...], 1 - slot)

    mild = mild_ref[slot] != 0

    @pl.when(mild)
    def _():
        all_heads(scores_one_product)

    @pl.when(jnp.logical_not(mild))
    def _():
        all_heads(scores_any_decay)


def _gla(q, k, vt, r, a, wg, bg, gh):
    t = q.shape[0]
    c = GLA_CHUNK
    const = lambda shape: pl.BlockSpec(shape, lambda i: (0, 0))
    last = t // c - 1
    return pl.pallas_call(
        _gla_kernel,
        grid=(t // c,),
        in_specs=[pl.BlockSpec((c, GLA_QK), lambda i: (i, 0)),
                  pl.BlockSpec((c, GLA_QK), lambda i: (i, 0)),
                  pl.BlockSpec((GLA_V, c), lambda i: (0, i)),
                  pl.BlockSpec((c, GLA_V), lambda i: (i, 0)),
                  pl.BlockSpec((c, LANES), lambda i: (i, 0)),
                  pl.BlockSpec((c, LANES), lambda i: (jnp.minimum(i + 1, last), 0)),
                  const((LANES, GLA_QK)),
                  const((1, GLA_QK)),
                  const((1, GLA_DV))],
        out_specs=pl.BlockSpec((c, GLA_V), lambda i: (i, 0)),
        out_shape=jax.ShapeDtypeStruct((t, GLA_V), BF16),
        scratch_shapes=[pltpu.VMEM((GLA_HEADS, GLA_DV, GLA_DK), F32),
                        pltpu.VMEM((2, c, GLA_QK), F32),
                        pltpu.SMEM((2,), jnp.int32)],
        compiler_params=_cparams(("arbitrary",)),
        name="gla_scan",
    )(q, k, vt, r, a, a, wg.astype(BF16), bg.reshape(1, GLA_QK), gh.reshape(1, GLA_DV))


def _sb_sum_matrix():
    tk = SB_TK
    n = SB_FUSED * tk
    j = jnp.arange(n)[:, None]
    s = jnp.arange(n)[None, :]
    later = (j > s)
    cols = [later[:, :tk], jnp.ones((n, tk), bool)] + [later[:, i * tk:(i + 1) * tk] for i in range(1, SB_FUSED)]
    return -jnp.concatenate(cols, axis=1).astype(BF16)


def _sb_kernel(q_ref, kt_ref, v_ref, u_ref, o_ref, acc_ref, carry_ref):
    tq, tk = SB_TQ, SB_TK
    rows = SB_GROUP * tq
    nf = SB_FUSED
    step = pl.program_id(1)
    qpos = lax.broadcasted_iota(jnp.int32, (rows, tk), 0) % tq
    kpos = lax.broadcasted_iota(jnp.int32, (rows, tk), 1)
    causal = kpos < qpos

    blocks = []
    for s in range(SB_QB):
        qs = jnp.concatenate([q_ref[s * tq:(s + 1) * tq, g * SB_DH:(g + 1) * SB_DH]
                              for g in range(SB_GROUP)], axis=0)
        blocks.append((s, step * SB_QB + s, qs))

    def tile(s, qs, kt, masked):
        k0 = pl.multiple_of(kt * tk, tk)
        z = _dot(qs, kt_ref[:, pl.ds(k0, tk)])
        sp = jnp.maximum(z, 0.0) + _log1p_exp_neg_abs(z)
        spm = jnp.where(causal, sp, 0.0) if masked else sp
        spb = spm.astype(BF16)
        tot = _dot(spb, u_ref[(nf - 1) * tk:, tk:2 * tk])
        tail = _dot(spb, u_ref[(nf - 1) * tk:, nf * tk:]) + carry_ref[s]
        w = jnp.exp((z - sp) + tail)
        if masked:
            w = jnp.where(causal, w, 0.0)
        acc_ref[s] += _dot(w.astype(BF16), v_ref[pl.ds(k0, tk), :])
        carry = carry_ref[s] + tot
        carry_ref[s] = carry
        return jnp.max(carry)

    def first_fused(s, qi, qs):
        k0 = pl.multiple_of((qi - (nf - 1)) * tk, tk)
        z = _dot(qs, kt_ref[:, pl.ds(k0, nf * tk)])
        sp = jnp.maximum(z, 0.0) + _log1p_exp_neg_abs(z)
        spm = jnp.concatenate([sp[:, :(nf - 1) * tk],
                               jnp.where(causal, sp[:, (nf - 1) * tk:], 0.0)], axis=1)
        spb = spm.astype(BF16)
        first = _dot(spb, u_ref[:, :2 * tk])
        later = _dot(spb[:, tk:], u_ref[tk:, 2 * tk:])
        tot = first[:, tk:]
        w = jnp.exp((z - sp) + jnp.concatenate([first[:, :tk], later], axis=1))
        w = jnp.concatenate([w[:, :(nf - 1) * tk],
                             jnp.where(causal, w[:, (nf - 1) * tk:], 0.0)], axis=1)
        acc_ref[s] = _dot(w.astype(BF16), v_ref[pl.ds(k0, nf * tk), :])
        carry_ref[s] = tot
        return qi - nf, jnp.max(tot)

    def first_single(s, qi, qs):
        acc_ref[s] = jnp.zeros((rows, SB_DH), F32)
        carry_ref[s] = jnp.zeros((rows, tk), F32)
        return qi - 1, tile(s, qs, qi, True)

    def all_fused():
        return tuple(first_fused(*blk) for blk in blocks)

    def all_single():
        return tuple(first_single(*blk) for blk in blocks)

    starts = lax.cond(step * SB_QB >= nf - 1, all_fused, all_single)

    def unfinished(state):
        kt, mx = state
        return jnp.logical_and(kt >= 0, mx > SB_EXIT)

    for (s, _, qs), start in zip(blocks, starts):
        def earlier_tile(state, s=s, qs=qs):
            kt, _ = state
            return kt - 1, tile(s, qs, kt, False)

        lax.while_loop(unfinished, earlier_tile, start)
        for g in range(SB_GROUP):
            o_ref[s * tq:(s + 1) * tq, g * SB_DH:(g + 1) * SB_DH] = (
                acc_ref[s, g * tq:(g + 1) * tq, :].astype(o_ref.dtype))


def _sb_attention(q, kt, v):
    t = q.shape[0]
    tq = SB_TQ * SB_QB
    gw = SB_GROUP * SB_DH
    u = _sb_sum_matrix()
    return pl.pallas_call(
        _sb_kernel,
        grid=(SB_KVH, t // tq),
        in_specs=[pl.BlockSpec((tq, gw), lambda n, i: (i, n)),
                  pl.BlockSpec((SB_DH, t), lambda n, i: (n, 0)),
                  pl.BlockSpec((t, SB_DH), lambda n, i: (0, n)),
                  pl.BlockSpec(u.shape, lambda n, i: (0, 0))],
        out_specs=pl.BlockSpec((tq, gw), lambda n, i: (i, n)),
        out_shape=jax.ShapeDtypeStruct((t, SB_HEADS * SB_DH), BF16),
        scratch_shapes=[pltpu.VMEM((SB_QB, SB_GROUP * SB_TQ, SB_DH), F32),
                        pltpu.VMEM((SB_QB, SB_GROUP * SB_TQ, SB_TK), F32)],
        compiler_params=_cparams(("arbitrary", "arbitrary")),
        name="stick_breaking",
    )(q, kt, v, u)


def _row_tile(t, want):
    return want if t % want == 0 else t


def kernel(x, norm1_g, norm2_g, gla_w_in, gla_w_gate_up, gla_b_gate, gla_head_g, gla_w_out,
           kv_norm_g, kv_w, sb_w_q, sb_w_out, mlp_w_up, mlp_w_down, final_g):
    b, t, d = x.shape
    assert b == 1 and d == D_MODEL and t % GLA_CHUNK == 0
    x2 = x.reshape(t, d)
    tm_big = _row_tile(t, 1024)
    tn = 1024
    tm = _row_tile(t, 512)
    tf = 1024

    o_k = GLA_QK
    o_v = 2 * GLA_QK
    o_r = o_v + GLA_V
    o_a = o_r + GLA_V
    w_in = gla_w_in[0].astype(BF16)
    w_a = jnp.pad(w_in[:, o_a:], ((0, 0), (0, LANES - GLA_RANK)))
    w_g = jnp.pad(gla_w_gate_up[0], ((0, LANES - GLA_RANK), (0, 0)))

    kv_w3 = kv_w.reshape(1, *kv_w.shape)

    q, xn, w_out0 = _norm_mm(x2, norm1_g[0], w_in, F32, tm=tm_big, cols=(0, GLA_QK),
                             casts=[(gla_w_out, 0)], name="gla_q")
    k, w_q1 = _mm(xn, w_in, F32, tm=tm_big, tn=tn, cols=(o_k, GLA_QK),
                  casts=[(sb_w_q, 0)], name="gla_k")
    vt, w_up0 = _mm(xn, w_in, BF16, tm=tm_big, tn=tn, cols=(o_v, GLA_V), transposed=True,
                    casts=[(mlp_w_up, 0)], name="gla_vt")
    r, w_down0 = _mm(xn, w_in, BF16, tm=tm_big, tn=tn, cols=(o_r, GLA_V),
                     casts=[(mlp_w_down, 0)], name="gla_r")
    a = _mm(xn, w_a, BF16, tm=tm_big, tn=LANES, name="gla_a")
    y = _gla(q, k, vt, r, a, w_g, gla_b_gate[0], gla_head_g[0])
    h, xn, w_out1, w_kv = _mm_res_norm(y, w_out0, x2, norm2_g[0], tm=tm,
                                       casts=[(sb_w_out, 0), (kv_w3, 0)], name="gla_out")
    h, xn_q, xn_kv = _mlp(xn, h, w_up0, w_down0, norm1_g[1], kv_norm_g,
                          tm=tm, tf=tf, final=False, name="mlp0")

    q2, w_up1 = _mm(xn_q, w_q1, BF16, tm=tm_big, tn=tn, scale=SB_DH ** -0.5,
                    casts=[(mlp_w_up, 1)], name="sb_q")
    kt = _mm(xn_kv, w_kv, BF16, tm=tm_big, tn=tn, cols=(0, SB_KV), transposed=True, name="shared_kt")
    v = _mm(xn_kv, w_kv, BF16, tm=tm_big, tn=tn, cols=(SB_KV, SB_KV), name="shared_v")
    o = _sb_attention(q2, kt, v)
    h, xn, w_down1 = _mm_res_norm(o, w_out1, h, norm2_g[1], tm=tm,
                                  casts=[(mlp_w_down, 1)], name="sb_out")
    out, = _mlp(xn, h, w_up1, w_down1, final_g, final_g, tm=tm, tf=tf, final=True, name="mlp1")
    return out.reshape(b, t, d)
```

```python
import functools

import jax
import jax.numpy as jnp
from jax import lax
from jax.experimental import pallas as pl
from jax.experimental.pallas import tpu as pltpu

F32 = jnp.float32
BF16 = jnp.bfloat16

EPS = 1e-6
D_MODEL = 2048
D_FF = 4 * D_MODEL

GLA_HEADS = 4
GLA_DK = 256
GLA_DV = 512
GLA_QK = GLA_HEADS * GLA_DK
GLA_V = GLA_HEADS * GLA_DV
GLA_RANK = 16
GLA_TAU = 16.0
GLA_CHUNK = 128
GLA_DIAG = 8
GLA_LEVELS = (64, 32, 16, 8)
GLA_SAFE_SPAN = 60.0

SB_HEADS = 16
SB_DH = 128
SB_KVH = 4
SB_GROUP = SB_HEADS // SB_KVH
SB_KV = SB_KVH * SB_DH
SB_TQ = 128
SB_TK = 128
SB_FUSED = 3
SB_QB = 4
SB_EXIT = -110.0

LANES = 128
BF16_SUBLANES = 16
VMEM_LIMIT = 56 * 1024 * 1024

NT_DIMS = (((1,), (1,)), ((), ()))
TN_NT_DIMS = (((0,), (1,)), ((), ()))


def _cparams(sem):
    return pltpu.CompilerParams(dimension_semantics=sem, vmem_limit_bytes=VMEM_LIMIT)


def _rms(x, g):
    return x * lax.rsqrt(jnp.mean(x * x, axis=-1, keepdims=True) + EPS) * g


def _split(x):
    hi = x.astype(BF16)
    lo = (x - hi.astype(F32)).astype(BF16)
    return hi, lo


def _dot(a, b):
    return jnp.dot(a, b, preferred_element_type=F32)


def _dot_nt(a, b):
    return lax.dot_general(a, b, NT_DIMS, preferred_element_type=F32)


def _log1p_exp_neg_abs(z):
    return jnp.log(1.0 + jnp.exp(-jnp.abs(z)))


def _cast_along(weights, grid):
    steps = 1
    for g in grid:
        steps *= g

    def step_index(*ids):
        flat = ids[0]
        for g, i in zip(grid[1:], ids[1:]):
            flat = flat * g + i
        return flat

    operands, in_specs, out_specs, out_shapes = [], [], [], []
    for arr, layer in weights:
        _, rows, cols = arr.shape
        slab = rows // steps
        assert slab * steps == rows and slab % BF16_SUBLANES == 0
        operands.append(arr)
        in_specs.append(pl.BlockSpec((None, slab, cols),
                                     lambda *ids, layer=layer: (layer, step_index(*ids), 0)))
        out_specs.append(pl.BlockSpec((slab, cols), lambda *ids: (step_index(*ids), 0)))
        out_shapes.append(jax.ShapeDtypeStruct((rows, cols), BF16))
    return operands, in_specs, out_specs, out_shapes


def _store_casts(src_refs, dst_refs):
    for src, dst in zip(src_refs, dst_refs):
        dst[...] = src[...].astype(dst.dtype)


def _mm_kernel(x_ref, w_ref, *refs, nt, scale, ncast):
    o_ref = refs[ncast]
    if nt:
        acc = lax.dot_general(w_ref[...], x_ref[...], TN_NT_DIMS,
                              preferred_element_type=F32)
    else:
        acc = _dot(x_ref[...], w_ref[...])
    if scale != 1.0:
        acc = acc * scale
    o_ref[...] = acc.astype(o_ref.dtype)
    _store_casts(refs[:ncast], refs[ncast + 1:])


def _mm_call(x, w, out_dtype, *, tm, tn, nt, j0, n, scale, casts, name):
    t, k = x.shape
    grid = (t // tm, n // tn)
    c_ops, c_in, c_out, c_shapes = _cast_along(casts, grid)
    w_spec = pl.BlockSpec((k, tn), lambda i, j: (0, j0 + j))
    if nt:
        o_spec = pl.BlockSpec((tn, tm), lambda i, j: (j, i))
        o_shape = jax.ShapeDtypeStruct((n, t), out_dtype)
    else:
        o_spec = pl.BlockSpec((tm, tn), lambda i, j: (i, j))
        o_shape = jax.ShapeDtypeStruct((t, n), out_dtype)
    outs = pl.pallas_call(
        functools.partial(_mm_kernel, nt=nt, scale=scale, ncast=len(casts)),
        grid=grid,
        in_specs=[pl.BlockSpec((tm, k), lambda i, j: (i, 0)), w_spec] + c_in,
        out_specs=[o_spec] + c_out,
        out_shape=[o_shape] + c_shapes,
        compiler_params=_cparams(("parallel", "arbitrary")),
        name=name,
    )(x, w, *c_ops)
    return outs if casts else outs[0]


def _mm(x, w, out_dtype, *, tm, tn, scale=1.0, cols=None, casts=(), transposed=False, name="proj"):
    c0, n = cols if cols is not None else (0, w.shape[1])
    tn = min(tn, n)
    assert c0 % tn == 0 and n % tn == 0
    return _mm_call(x, w, out_dtype, tm=tm, tn=tn, nt=transposed, j0=c0 // tn, n=n, scale=scale,
                    casts=casts, name=name)


def _norm_mm_kernel(x_ref, g_ref, w_ref, *refs, ncast):
    o_ref, xn_ref = refs[ncast:ncast + 2]
    xn = _rms(x_ref[...], g_ref[...]).astype(xn_ref.dtype)
    xn_ref[...] = xn
    o_ref[...] = _dot(xn, w_ref[...]).astype(o_ref.dtype)
    _store_casts(refs[:ncast], refs[ncast + 2:])


def _norm_mm(x, g, w, out_dtype, *, tm, cols, casts=(), name):
    t, k = x.shape
    c0, n = cols
    assert c0 % n == 0
    grid = (t // tm,)
    c_ops, c_in, c_out, c_shapes = _cast_along(casts, grid)
    return pl.pallas_call(
        functools.partial(_norm_mm_kernel, ncast=len(casts)),
        grid=grid,
        in_specs=[pl.BlockSpec((tm, k), lambda i: (i, 0)),
                  pl.BlockSpec((1, k), lambda i: (0, 0)),
                  pl.BlockSpec((k, n), lambda i: (0, c0 // n))] + c_in,
        out_specs=[pl.BlockSpec((tm, n), lambda i: (i, 0)),
                   pl.BlockSpec((tm, k), lambda i: (i, 0))] + c_out,
        out_shape=[jax.ShapeDtypeStruct((t, n), out_dtype),
                   jax.ShapeDtypeStruct((t, k), BF16)] + c_shapes,
        compiler_params=_cparams(("parallel",)),
        name=name,
    )(x, g.reshape(1, k), w, *c_ops)


def _mm_res_norm_kernel(y_ref, w_ref, res_ref, g_ref, *refs, ncast):
    h_ref, xn_ref = refs[ncast:ncast + 2]
    h = res_ref[...] + _dot(y_ref[...], w_ref[...])
    h_ref[...] = h
    xn_ref[...] = _rms(h, g_ref[...]).astype(xn_ref.dtype)
    _store_casts(refs[:ncast], refs[ncast + 2:])


def _mm_res_norm(y, w, res, g, *, tm, casts=(), name):
    t, k = y.shape
    n = w.shape[1]
    grid = (t // tm,)
    c_ops, c_in, c_out, c_shapes = _cast_along(casts, grid)
    return pl.pallas_call(
        functools.partial(_mm_res_norm_kernel, ncast=len(casts)),
        grid=grid,
        in_specs=[pl.BlockSpec((tm, k), lambda i: (i, 0)),
                  pl.BlockSpec((k, n), lambda i: (0, 0)),
                  pl.BlockSpec((tm, n), lambda i: (i, 0)),
                  pl.BlockSpec((1, n), lambda i: (0, 0))] + c_in,
        out_specs=[pl.BlockSpec((tm, n), lambda i: (i, 0)),
                   pl.BlockSpec((tm, n), lambda i: (i, 0))] + c_out,
        out_shape=[jax.ShapeDtypeStruct((t, n), F32),
                   jax.ShapeDtypeStruct((t, n), BF16)] + c_shapes,
        compiler_params=_cparams(("parallel",)),
        name=name,
    )(y, w, res, g.reshape(1, n), *c_ops)


def _mlp_kernel(xn_ref, h_ref, wu_ref, wd_ref, g1_ref, g2_ref, *refs, final, ncast):
    n_out = 1 if final else 3
    acc_ref = refs[ncast]
    norm_refs = refs[ncast + 1:ncast + n_out]
    _store_casts(refs[:ncast], refs[ncast + n_out:])
    f = pl.program_id(1)

    @pl.when(f == 0)
    def _():
        acc_ref[...] = h_ref[...]

    u = _dot(xn_ref[...], wu_ref[...])
    a = jnp.square(jnp.maximum(u, 0.0)).astype(BF16)
    acc_ref[...] += _dot(a, wd_ref[...])

    @pl.when(f == pl.num_programs(1) - 1)
    def _():
        h = acc_ref[...]
        hn = h * lax.rsqrt(jnp.mean(h * h, axis=-1, keepdims=True) + EPS)
        if final:
            acc_ref[...] = hn * g1_ref[...]
        else:
            xa_ref, xb_ref = norm_refs
            xa_ref[...] = (hn * g1_ref[...]).astype(BF16)
            xb_ref[...] = (hn * g2_ref[...]).astype(BF16)


def _mlp(xn, h, wu, wd, g1, g2, *, tm, tf, final, casts=(), name):
    t, d = xn.shape
    ff = wu.shape[1]
    grid = (t // tm, ff // tf)
    c_ops, c_in, c_out, c_shapes = _cast_along(casts, grid)
    row = pl.BlockSpec((tm, d), lambda i, f: (i, 0))
    vec = pl.BlockSpec((1, d), lambda i, f: (0, 0))
    out_specs = [row]
    out_shape = [jax.ShapeDtypeStruct((t, d), F32)]
    if not final:
        out_specs += [row, row]
        out_shape += [jax.ShapeDtypeStruct((t, d), BF16), jax.ShapeDtypeStruct((t, d), BF16)]
    return pl.pallas_call(
        functools.partial(_mlp_kernel, final=final, ncast=len(casts)),
        grid=grid,
        in_specs=[row, row,
                  pl.BlockSpec((d, tf), lambda i, f: (0, f)),
                  pl.BlockSpec((tf, d), lambda i, f: (f, 0)),
                  vec, vec] + c_in,
        out_specs=out_specs + c_out,
        out_shape=out_shape + c_shapes,
        compiler_params=_cparams(("parallel", "arbitrary")),
        name=name,
    )(xn, h, wu, wd, g1.reshape(1, d), g2.reshape(1, d), *c_ops)


def _gla_kernel(q_ref, k_ref, vt_ref, r_ref, a_ref, a_next_ref, wg_ref, bg_ref, gh_ref,
                y_ref, st_ref, cum2_ref, mild_ref):
    c = GLA_CHUNK
    nd = GLA_DIAG
    step = pl.program_id(0)
    slot = step % 2
    cum_ref = cum2_ref.at[slot]
    ri = lax.broadcasted_iota(jnp.int32, (c, c), 0)
    ci = lax.broadcasted_iota(jnp.int32, (c, c), 1)

    def gate(a, dst):
        z = _dot(a, wg_ref[...]) + bg_ref[...]
        log_alpha = (jnp.minimum(z, 0.0) - _log1p_exp_neg_abs(z)) * (1.0 / GLA_TAU)
        tri = (ci <= ri).astype(BF16)
        la_hi, la_lo = _split(log_alpha)
        cum = _dot(tri, la_hi) + _dot(tri, la_lo)
        cum2_ref[dst] = cum
        mild_ref[dst] = (jnp.max(-cum[c - 1:c, :]) <= GLA_SAFE_SPAN).astype(jnp.int32)

    @pl.when(step == 0)
    def _():
        st_ref[...] = jnp.zeros_like(st_ref)
        gate(a_ref[...], 0)

    row_k = lax.broadcasted_iota(jnp.int32, (c, GLA_DK), 0)
    row_d = lax.broadcasted_iota(jnp.int32, (nd, GLA_DK), 0)
    lane_d = lax.broadcasted_iota(jnp.int32, (nd, c), 1)

    def scores_one_product(ks, ch, kh, qs, qe):
        k_inv = (kh * jnp.exp(-ch)).astype(BF16)
        return jnp.where(ci <= ri, _dot_nt(qe, k_inv), 0.0)

    def scores_any_decay(ks, ch, kh, qs, qe):
        p = jnp.zeros((c, c), F32)
        for b in GLA_LEVELS:
            upper = (row_k % (2 * b)) >= b
            same_group = (ri // (2 * b)) == (ci // (2 * b))
            ref = jnp.concatenate(
                [jnp.broadcast_to(cum_ref[g + b:g + b + 1, ks], (2 * b, GLA_DK))
                 for g in range(0, c, 2 * b)], axis=0)
            e = jnp.exp(-jnp.abs(ch - ref))
            q_sc = jnp.where(upper, qs * e, 0.0).astype(BF16)
            k_sc = jnp.where(upper, 0.0, kh * e).astype(BF16)
            p = p + jnp.where(same_group, _dot_nt(q_sc, k_sc), 0.0)
        p_rows = []
        for r0 in range(0, c, nd):
            c_i = ch[r0:r0 + nd]
            q_i = qs[r0:r0 + nd]
            p_i = p[r0:r0 + nd]
            for j in range(nd):
                c_j = cum_ref[r0 + j:r0 + j + 1, ks]
                k_j = k_ref[r0 + j:r0 + j + 1, ks]
                dec = jnp.exp(jnp.where(row_d >= j, c_i - c_j, -jnp.inf))
                col = jnp.sum(q_i * k_j * dec, axis=1, keepdims=True)
                p_i = jnp.where(lane_d == r0 + j, col, p_i)
            p_rows.append(p_i)
        return jnp.concatenate(p_rows, axis=0)

    def all_heads(scores):
        for h in range(GLA_HEADS):
            ks = slice(h * GLA_DK, (h + 1) * GLA_DK)
            vs = slice(h * GLA_DV, (h + 1) * GLA_DV)
            ch = cum_ref[:, ks]
            kh = k_ref[:, ks]
            qs = q_ref[:, ks] * (GLA_DK ** -0.5)
            vt = vt_ref[vs, :]
            st = st_ref[h]

            qe = (qs * jnp.exp(ch)).astype(BF16)
            o = _dot_nt(qe, st.astype(BF16))
            p = scores(ks, ch, kh, qs, qe)
            o = o + _dot_nt(p.astype(BF16), vt)

            last = cum_ref[c - 1:c, ks]
            kd = (kh * jnp.exp(last - ch)).astype(BF16)
            st_ref[h] = st * jnp.exp(last) + _dot(vt, kd)

            on = _rms(o, gh_ref[...])
            rr = r_ref[:, vs].astype(F32)
            y_ref[:, vs] = (on * (rr * jax.nn.sigmoid(rr))).astype(y_ref.dtype)
        gate(a_next_ref[...], 1 - slot)

    mild = mild_ref[slot] != 0

    @pl.when(mild)
    def _():
        all_heads(scores_one_product)

    @pl.when(jnp.logical_not(mild))
    def _():
        all_heads(scores_any_decay)


def _gla(q, k, vt, r, a, wg, bg, gh):
    t = q.shape[0]
    c = GLA_CHUNK
    const = lambda shape: pl.BlockSpec(shape, lambda i: (0, 0))
    last = t // c - 1
    return pl.pallas_call(
        _gla_kernel,
        grid=(t // c,),
        in_specs=[pl.BlockSpec((c, GLA_QK), lambda i: (i, 0)),
                  pl.BlockSpec((c, GLA_QK), lambda i: (i, 0)),
                  pl.BlockSpec((GLA_V, c), lambda i: (0, i)),
                  pl.BlockSpec((c, GLA_V), lambda i: (i, 0)),
                  pl.BlockSpec((c, LANES), lambda i: (i, 0)),
                  pl.BlockSpec((c, LANES), lambda i: (jnp.minimum(i + 1, last), 0)),
                  const((LANES, GLA_QK)),
                  const((1, GLA_QK)),
                  const((1, GLA_DV))],
        out_specs=pl.BlockSpec((c, GLA_V), lambda i: (i, 0)),
        out_shape=jax.ShapeDtypeStruct((t, GLA_V), BF16),
        scratch_shapes=[pltpu.VMEM((GLA_HEADS, GLA_DV, GLA_DK), F32),
                        pltpu.VMEM((2, c, GLA_QK), F32),
                        pltpu.SMEM((2,), jnp.int32)],
        compiler_params=_cparams(("arbitrary",)),
        name="gla_scan",
    )(q, k, vt, r, a, a, wg.astype(BF16), bg.reshape(1, GLA_QK), gh.reshape(1, GLA_DV))


def _sb_sum_matrix():
    tk = SB_TK
    n = SB_FUSED * tk
    j = jnp.arange(n)[:, None]
    s = jnp.arange(n)[None, :]
    later = (j > s)
    cols = [later[:, :tk], jnp.ones((n, tk), bool)] + [later[:, i * tk:(i + 1) * tk] for i in range(1, SB_FUSED)]
    return -jnp.concatenate(cols, axis=1).astype(BF16)


def _sb_kernel(q_ref, kt_ref, v_ref, u_ref, o_ref, acc_ref, carry_ref):
    tq, tk = SB_TQ, SB_TK
    rows = SB_GROUP * tq
    nf = SB_FUSED
    step = pl.program_id(1)
    qpos = lax.broadcasted_iota(jnp.int32, (rows, tk), 0) % tq
    kpos = lax.broadcasted_iota(jnp.int32, (rows, tk), 1)
    causal = kpos < qpos

    blocks = []
    for s in range(SB_QB):
        qs = jnp.concatenate([q_ref[s * tq:(s + 1) * tq, g * SB_DH:(g + 1) * SB_DH]
                              for g in range(SB_GROUP)], axis=0)
        blocks.append((s, step * SB_QB + s, qs))

    def tile(s, qs, kt, masked):
        k0 = pl.multiple_of(kt * tk, tk)
        z = _dot(qs, kt_ref[:, pl.ds(k0, tk)])
        sp = jnp.maximum(z, 0.0) + _log1p_exp_neg_abs(z)
        spm = jnp.where(causal, sp, 0.0) if masked else sp
        spb = spm.astype(BF16)
        tot = _dot(spb, u_ref[(nf - 1) * tk:, tk:2 * tk])
        tail = _dot(spb, u_ref[(nf - 1) * tk:, nf * tk:]) + carry_ref[s]
        w = jnp.exp((z - sp) + tail)
        if masked:
            w = jnp.where(causal, w, 0.0)
        acc_ref[s] += _dot(w.astype(BF16), v_ref[pl.ds(k0, tk), :])
        carry = carry_ref[s] + tot
        carry_ref[s] = carry
        return jnp.max(carry)

    def first_fused(s, qi, qs):
        k0 = pl.multiple_of((qi - (nf - 1)) * tk, tk)
        z = _dot(qs, kt_ref[:, pl.ds(k0, nf * tk)])
        sp = jnp.maximum(z, 0.0) + _log1p_exp_neg_abs(z)
        spm = jnp.concatenate([sp[:, :(nf - 1) * tk],
                               jnp.where(causal, sp[:, (nf - 1) * tk:], 0.0)], axis=1)
        spb = spm.astype(BF16)
        first = _dot(spb, u_ref[:, :2 * tk])
        later = _dot(spb[:, tk:], u_ref[tk:, 2 * tk:])
        tot = first[:, tk:]
        w = jnp.exp((z - sp) + jnp.concatenate([first[:, :tk], later], axis=1))
        w = jnp.concatenate([w[:, :(nf - 1) * tk],
                             jnp.where(causal, w[:, (nf - 1) * tk:], 0.0)], axis=1)
        acc_ref[s] = _dot(w.astype(BF16), v_ref[pl.ds(k0, nf * tk), :])
        carry_ref[s] = tot
        return qi - nf, jnp.max(tot)

    def first_single(s, qi, qs):
        acc_ref[s] = jnp.zeros((rows, SB_DH), F32)
        carry_ref[s] = jnp.zeros((rows, tk), F32)
        return qi - 1, tile(s, qs, qi, True)

    def all_fused():
        return tuple(first_fused(*blk) for blk in blocks)

    def all_single():
        return tuple(first_single(*blk) for blk in blocks)

    starts = lax.cond(step * SB_QB >= nf - 1, all_fused, all_single)

    def unfinished(state):
        kt, mx = state
        return jnp.logical_and(kt >= 0, mx > SB_EXIT)

    for (s, _, qs), start in zip(blocks, starts):
        def earlier_tile(state, s=s, qs=qs):
            kt, _ = state
            return kt - 1, tile(s, qs, kt, False)

        lax.while_loop(unfinished, earlier_tile, start)
        for g in range(SB_GROUP):
            o_ref[s * tq:(s + 1) * tq, g * SB_DH:(g + 1) * SB_DH] = (
                acc_ref[s, g * tq:(g + 1) * tq, :].astype(o_ref.dtype))


def _sb_attention(q, kt, v):
    t = q.shape[0]
    tq = SB_TQ * SB_QB
    gw = SB_GROUP * SB_DH
    u = _sb_sum_matrix()
    return pl.pallas_call(
        _sb_kernel,
        grid=(SB_KVH, t // tq),
        in_specs=[pl.BlockSpec((tq, gw), lambda n, i: (i, n)),
                  pl.BlockSpec((SB_DH, t), lambda n, i: (n, 0)),
                  pl.BlockSpec((t, SB_DH), lambda n, i: (0, n)),
                  pl.BlockSpec(u.shape, lambda n, i: (0, 0))],
        out_specs=pl.BlockSpec((tq, gw), lambda n, i: (i, n)),
        out_shape=jax.ShapeDtypeStruct((t, SB_HEADS * SB_DH), BF16),
        scratch_shapes=[pltpu.VMEM((SB_QB, SB_GROUP * SB_TQ, SB_DH), F32),
                        pltpu.VMEM((SB_QB, SB_GROUP * SB_TQ, SB_TK), F32)],
        compiler_params=_cparams(("arbitrary", "arbitrary")),
        name="stick_breaking",
    )(q, kt, v, u)


def _row_tile(t, want):
    return want if t % want == 0 else t


def kernel(x, norm1_g, norm2_g, gla_w_in, gla_w_gate_up, gla_b_gate, gla_head_g, gla_w_out,
           kv_norm_g, kv_w, sb_w_q, sb_w_out, mlp_w_up, mlp_w_down, final_g):
    b, t, d = x.shape
    assert b == 1 and d == D_MODEL and t % GLA_CHUNK == 0
    x2 = x.reshape(t, d)
    tm_big = _row_tile(t, 1024)
    tn = 2048
    tm = _row_tile(t, 512)
    tf = 1024

    o_k = GLA_QK
    o_v = 2 * GLA_QK
    o_r = o_v + GLA_V
    o_a = o_r + GLA_V
    w_in = gla_w_in[0].astype(BF16)
    w_a = jnp.pad(w_in[:, o_a:], ((0, 0), (0, LANES - GLA_RANK)))
    w_g = jnp.pad(gla_w_gate_up[0], ((0, LANES - GLA_RANK), (0, 0)))

    kv_w3 = kv_w.reshape(1, *kv_w.shape)

    q, xn, w_out0 = _norm_mm(x2, norm1_g[0], w_in, F32, tm=tm_big, cols=(0, GLA_QK),
                             casts=[(gla_w_out, 0)], name="gla_q")
    k, w_q1 = _mm(xn, w_in, F32, tm=tm_big, tn=tn, cols=(o_k, GLA_QK),
                  casts=[(sb_w_q, 0)], name="gla_k")
    vt, w_up0 = _mm(xn, w_in, BF16, tm=tm_big, tn=tn, cols=(o_v, GLA_V), transposed=True,
                    casts=[(mlp_w_up, 0)], name="gla_vt")
    r, w_down0 = _mm(xn, w_in, BF16, tm=tm_big, tn=tn, cols=(o_r, GLA_V),
                     casts=[(mlp_w_down, 0)], name="gla_r")
    a = _mm(xn, w_a, BF16, tm=tm_big, tn=LANES, name="gla_a")
    y = _gla(q, k, vt, r, a, w_g, gla_b_gate[0], gla_head_g[0])
    h, xn, w_out1, w_kv = _mm_res_norm(y, w_out0, x2, norm2_g[0], tm=tm,
                                       casts=[(sb_w_out, 0), (kv_w3, 0)], name="gla_out")
    h, xn_q, xn_kv = _mlp(xn, h, w_up0, w_down0, norm1_g[1], kv_norm_g,
                          tm=tm, tf=tf, final=False, name="mlp0")

    q2, w_up1 = _mm(xn_q, w_q1, BF16, tm=tm_big, tn=tn, scale=SB_DH ** -0.5,
                    casts=[(mlp_w_up, 1)], name="sb_q")
    kt = _mm(xn_kv, w_kv, BF16, tm=tm_big, tn=tn, cols=(0, SB_KV), transposed=True, name="shared_kt")
    v = _mm(xn_kv, w_kv, BF16, tm=tm_big, tn=tn, cols=(SB_KV, SB_KV), name="shared_v")
    o = _sb_attention(q2, kt, v)
    h, xn, w_down1 = _mm_res_norm(o, w_out1, h, norm2_g[1], tm=tm,
                                  casts=[(mlp_w_down, 1)], name="sb_out")
    out, = _mlp(xn, h, w_up1, w_down1, final_g, final_g, tm=tm, tf=tf, final=True, name="mlp1")
    return out.reshape(b, t, d)
```

```python
import functools

import jax
import jax.numpy as jnp
from jax import lax
from jax.experimental import pallas as pl
from jax.experimental.pallas import tpu as pltpu

F32 = jnp.float32
BF16 = jnp.bfloat16

EPS = 1e-6
D_MODEL = 2048
D_FF = 4 * D_MODEL

GLA_HEADS = 4
GLA_DK = 256
GLA_DV = 512
GLA_QK = GLA_HEADS * GLA_DK
GLA_V = GLA_HEADS * GLA_DV
GLA_RANK = 16
GLA_TAU = 16.0
GLA_CHUNK = 128
GLA_DIAG = 8
GLA_LEVELS = (64, 32, 16, 8)
GLA_SAFE_SPAN = 60.0

SB_HEADS = 16
SB_DH = 128
SB_KVH = 4
SB_GROUP = SB_HEADS // SB_KVH
SB_KV = SB_KVH * SB_DH
SB_TQ = 128
SB_TK = 128
SB_FUSED = 3
SB_QB = 8
SB_EXIT = -110.0

LANES = 128
BF16_SUBLANES = 16
VMEM_LIMIT = 56 * 1024 * 1024

NT_DIMS = (((1,), (1,)), ((), ()))
TN_NT_DIMS = (((0,), (1,)), ((), ()))


def _cparams(sem):
    return pltpu.CompilerParams(dimension_semantics=sem, vmem_limit_bytes=VMEM_LIMIT)


def _rms(x, g):
    return x * lax.rsqrt(jnp.mean(x * x, axis=-1, keepdims=True) + EPS) * g


def _split(x):
    hi = x.astype(BF16)
    lo = (x - hi.astype(F32)).astype(BF16)
    return hi, lo


def _dot(a, b):
    return jnp.dot(a, b, preferred_element_type=F32)


def _dot_nt(a, b):
    return lax.dot_general(a, b, NT_DIMS, preferred_element_type=F32)


def _log1p_exp_neg_abs(z):
    return jnp.log(1.0 + jnp.exp(-jnp.abs(z)))


def _cast_along(weights, grid):
    steps = 1
    for g in grid:
        steps *= g

    def step_index(*ids):
        flat = ids[0]
        for g, i in zip(grid[1:], ids[1:]):
            flat = flat * g + i
        return flat

    operands, in_specs, out_specs, out_shapes = [], [], [], []
    for arr, layer in weights:
        _, rows, cols = arr.shape
        slab = rows // steps
        assert slab * steps == rows and slab % BF16_SUBLANES == 0
        operands.append(arr)
        in_specs.append(pl.BlockSpec((None, slab, cols),
                                     lambda *ids, layer=layer: (layer, step_index(*ids), 0)))
        out_specs.append(pl.BlockSpec((slab, cols), lambda *ids: (step_index(*ids), 0)))
        out_shapes.append(jax.ShapeDtypeStruct((rows, cols), BF16))
    return operands, in_specs, out_specs, out_shapes


def _store_casts(src_refs, dst_refs):
    for src, dst in zip(src_refs, dst_refs):
        dst[...] = src[...].astype(dst.dtype)


def _mm_kernel(x_ref, w_ref, *refs, nt, scale, ncast):
    o_ref = refs[ncast]
    if nt:
        acc = lax.dot_general(w_ref[...], x_ref[...], TN_NT_DIMS,
                              preferred_element_type=F32)
    else:
        acc = _dot(x_ref[...], w_ref[...])
    if scale != 1.0:
        acc = acc * scale
    o_ref[...] = acc.astype(o_ref.dtype)
    _store_casts(refs[:ncast], refs[ncast + 1:])


def _mm_call(x, w, out_dtype, *, tm, tn, nt, j0, n, scale, casts, name):
    t, k = x.shape
    grid = (t // tm, n // tn)
    c_ops, c_in, c_out, c_shapes = _cast_along(casts, grid)
    w_spec = pl.BlockSpec((k, tn), lambda i, j: (0, j0 + j))
    if nt:
        o_spec = pl.BlockSpec((tn, tm), lambda i, j: (j, i))
        o_shape = jax.ShapeDtypeStruct((n, t), out_dtype)
    else:
        o_spec = pl.BlockSpec((tm, tn), lambda i, j: (i, j))
        o_shape = jax.ShapeDtypeStruct((t, n), out_dtype)
    outs = pl.pallas_call(
        functools.partial(_mm_kernel, nt=nt, scale=scale, ncast=len(casts)),
        grid=grid,
        in_specs=[pl.BlockSpec((tm, k), lambda i, j: (i, 0)), w_spec] + c_in,
        out_specs=[o_spec] + c_out,
        out_shape=[o_shape] + c_shapes,
        compiler_params=_cparams(("parallel", "arbitrary")),
        name=name,
    )(x, w, *c_ops)
    return outs if casts else outs[0]


def _mm(x, w, out_dtype, *, tm, tn, scale=1.0, cols=None, casts=(), transposed=False, name="proj"):
    c0, n = cols if cols is not None else (0, w.shape[1])
    tn = min(tn, n)
    assert c0 % tn == 0 and n % tn == 0
    return _mm_call(x, w, out_dtype, tm=tm, tn=tn, nt=transposed, j0=c0 // tn, n=n, scale=scale,
                    casts=casts, name=name)


def _norm_mm_kernel(x_ref, g_ref, w_ref, *refs, ncast):
    o_ref, xn_ref = refs[ncast:ncast + 2]
    xn = _rms(x_ref[...], g_ref[...]).astype(xn_ref.dtype)
    xn_ref[...] = xn
    o_ref[...] = _dot(xn, w_ref[...]).astype(o_ref.dtype)
    _store_casts(refs[:ncast], refs[ncast + 2:])


def _norm_mm(x, g, w, out_dtype, *, tm, cols, casts=(), name):
    t, k = x.shape
    c0, n = cols
    assert c0 % n == 0
    grid = (t // tm,)
    c_ops, c_in, c_out, c_shapes = _cast_along(casts, grid)
    return pl.pallas_call(
        functools.partial(_norm_mm_kernel, ncast=len(casts)),
        grid=grid,
        in_specs=[pl.BlockSpec((tm, k), lambda i: (i, 0)),
                  pl.BlockSpec((1, k), lambda i: (0, 0)),
                  pl.BlockSpec((k, n), lambda i: (0, c0 // n))] + c_in,
        out_specs=[pl.BlockSpec((tm, n), lambda i: (i, 0)),
                   pl.BlockSpec((tm, k), lambda i: (i, 0))] + c_out,
        out_shape=[jax.ShapeDtypeStruct((t, n), out_dtype),
                   jax.ShapeDtypeStruct((t, k), BF16)] + c_shapes,
        compiler_params=_cparams(("parallel",)),
        name=name,
    )(x, g.reshape(1, k), w, *c_ops)


def _mm_res_norm_kernel(y_ref, w_ref, res_ref, g_ref, *refs, ncast):
    h_ref, xn_ref = refs[ncast:ncast + 2]
    h = res_ref[...] + _dot(y_ref[...], w_ref[...])
    h_ref[...] = h
    xn_ref[...] = _rms(h, g_ref[...]).astype(xn_ref.dtype)
    _store_casts(refs[:ncast], refs[ncast + 2:])


def _mm_res_norm(y, w, res, g, *, tm, casts=(), name):
    t, k = y.shape
    n = w.shape[1]
    grid = (t // tm,)
    c_ops, c_in, c_out, c_shapes = _cast_along(casts, grid)
    return pl.pallas_call(
        functools.partial(_mm_res_norm_kernel, ncast=len(casts)),
        grid=grid,
        in_specs=[pl.BlockSpec((tm, k), lambda i: (i, 0)),
                  pl.BlockSpec((k, n), lambda i: (0, 0)),
                  pl.BlockSpec((tm, n), lambda i: (i, 0)),
                  pl.BlockSpec((1, n), lambda i: (0, 0))] + c_in,
        out_specs=[pl.BlockSpec((tm, n), lambda i: (i, 0)),
                   pl.BlockSpec((tm, n), lambda i: (i, 0))] + c_out,
        out_shape=[jax.ShapeDtypeStruct((t, n), F32),
                   jax.ShapeDtypeStruct((t, n), BF16)] + c_shapes,
        compiler_params=_cparams(("parallel",)),
        name=name,
    )(y, w, res, g.reshape(1, n), *c_ops)


def _mlp_kernel(xn_ref, h_ref, wu_ref, wd_ref, g1_ref, g2_ref, *refs, final, ncast):
    n_out = 1 if final else 3
    acc_ref = refs[ncast]
    norm_refs = refs[ncast + 1:ncast + n_out]
    _store_casts(refs[:ncast], refs[ncast + n_out:])
    f = pl.program_id(1)

    @pl.when(f == 0)
    def _():
        acc_ref[...] = h_ref[...]

    u = _dot(xn_ref[...], wu_ref[...])
    a = jnp.square(jnp.maximum(u, 0.0)).astype(BF16)
    acc_ref[...] += _dot(a, wd_ref[...])

    @pl.when(f == pl.num_programs(1) - 1)
    def _():
        h = acc_ref[...]
        hn = h * lax.rsqrt(jnp.mean(h * h, axis=-1, keepdims=True) + EPS)
        if final:
            acc_ref[...] = hn * g1_ref[...]
        else:
            xa_ref, xb_ref = norm_refs
            xa_ref[...] = (hn * g1_ref[...]).astype(BF16)
            xb_ref[...] = (hn * g2_ref[...]).astype(BF16)


def _mlp(xn, h, wu, wd, g1, g2, *, tm, tf, final, casts=(), name):
    t, d = xn.shape
    ff = wu.shape[1]
    grid = (t // tm, ff // tf)
    c_ops, c_in, c_out, c_shapes = _cast_along(casts, grid)
    row = pl.BlockSpec((tm, d), lambda i, f: (i, 0))
    vec = pl.BlockSpec((1, d), lambda i, f: (0, 0))
    out_specs = [row]
    out_shape = [jax.ShapeDtypeStruct((t, d), F32)]
    if not final:
        out_specs += [row, row]
        out_shape += [jax.ShapeDtypeStruct((t, d), BF16), jax.ShapeDtypeStruct((t, d), BF16)]
    return pl.pallas_call(
        functools.partial(_mlp_kernel, final=final, ncast=len(casts)),
        grid=grid,
        in_specs=[row, row,
                  pl.BlockSpec((d, tf), lambda i, f: (0, f)),
                  pl.BlockSpec((tf, d), lambda i, f: (f, 0)),
                  vec, vec] + c_in,
        out_specs=out_specs + c_out,
        out_shape=out_shape + c_shapes,
        compiler_params=_cparams(("parallel", "arbitrary")),
        name=name,
    )(xn, h, wu, wd, g1.reshape(1, d), g2.reshape(1, d), *c_ops)


def _gla_kernel(q_ref, k_ref, vt_ref, r_ref, a_ref, a_next_ref, wg_ref, bg_ref, gh_ref,
                y_ref, st_ref, cum2_ref, mild_ref):
    c = GLA_CHUNK
    nd = GLA_DIAG
    step = pl.program_id(0)
    slot = step % 2
    cum_ref = cum2_ref.at[slot]
    ri = lax.broadcasted_iota(jnp.int32, (c, c), 0)
    ci = lax.broadcasted_iota(jnp.int32, (c, c), 1)

    def gate(a, dst):
        z = _dot(a, wg_ref[...]) + bg_ref[...]
        log_alpha = (jnp.minimum(z, 0.0) - _log1p_exp_neg_abs(z)) * (1.0 / GLA_TAU)
        tri = (ci <= ri).astype(BF16)
        la_hi, la_lo = _split(log_alpha)
        cum = _dot(tri, la_hi) + _dot(tri, la_lo)
        cum2_ref[dst] = cum
        mild_ref[dst] = (jnp.max(-cum[c - 1:c, :]) <= GLA_SAFE_SPAN).astype(jnp.int32)

    @pl.when(step == 0)
    def _():
        st_ref[...] = jnp.zeros_like(st_ref)
        gate(a_ref[...], 0)

    row_k = lax.broadcasted_iota(jnp.int32, (c, GLA_DK), 0)
    row_d = lax.broadcasted_iota(jnp.int32, (nd, GLA_DK), 0)
    lane_d = lax.broadcasted_iota(jnp.int32, (nd, c), 1)

    def scores_one_product(ks, ch, kh, qs, qe):
        k_inv = (kh * jnp.exp(-ch)).astype(BF16)
        return jnp.where(ci <= ri, _dot_nt(qe, k_inv), 0.0)

    def scores_any_decay(ks, ch, kh, qs, qe):
        p = jnp.zeros((c, c), F32)
        for b in GLA_LEVELS:
            upper = (row_k % (2 * b)) >= b
            same_group = (ri // (2 * b)) == (ci // (2 * b))
            ref = jnp.concatenate(
                [jnp.broadcast_to(cum_ref[g + b:g + b + 1, ks], (2 * b, GLA_DK))
                 for g in range(0, c, 2 * b)], axis=0)
            e = jnp.exp(-jnp.abs(ch - ref))
            q_sc = jnp.where(upper, qs * e, 0.0).astype(BF16)
            k_sc = jnp.where(upper, 0.0, kh * e).astype(BF16)
            p = p + jnp.where(same_group, _dot_nt(q_sc, k_sc), 0.0)
        p_rows = []
        for r0 in range(0, c, nd):
            c_i = ch[r0:r0 + nd]
            q_i = qs[r0:r0 + nd]
            p_i = p[r0:r0 + nd]
            for j in range(nd):
                c_j = cum_ref[r0 + j:r0 + j + 1, ks]
                k_j = k_ref[r0 + j:r0 + j + 1, ks]
                dec = jnp.exp(jnp.where(row_d >= j, c_i - c_j, -jnp.inf))
                col = jnp.sum(q_i * k_j * dec, axis=1, keepdims=True)
                p_i = jnp.where(lane_d == r0 + j, col, p_i)
            p_rows.append(p_i)
        return jnp.concatenate(p_rows, axis=0)

    def all_heads(scores):
        for h in range(GLA_HEADS):
            ks = slice(h * GLA_DK, (h + 1) * GLA_DK)
            vs = slice(h * GLA_DV, (h + 1) * GLA_DV)
            ch = cum_ref[:, ks]
            kh = k_ref[:, ks]
            qs = q_ref[:, ks] * (GLA_DK ** -0.5)
            vt = vt_ref[vs, :]
            st = st_ref[h]

            qe = (qs * jnp.exp(ch)).astype(BF16)
            o = _dot_nt(qe, st.astype(BF16))
            p = scores(ks, ch, kh, qs, qe)
            o = o + _dot_nt(p.astype(BF16), vt)

            last = cum_ref[c - 1:c, ks]
            kd = (kh * jnp.exp(last - ch)).astype(BF16)
            st_ref[h] = st * jnp.exp(last) + _dot(vt, kd)

            on = _rms(o, gh_ref[...])
            rr = r_ref[:, vs].astype(F32)
            y_ref[:, vs] = (on * (rr * jax.nn.sigmoid(rr))).astype(y_ref.dtype)
        gate(a_next_ref[...], 1 - slot)

    mild = mild_ref[slot] != 0

    @pl.when(mild)
    def _():
        all_heads(scores_one_product)

    @pl.when(jnp.logical_not(mild))
    def _():
        all_heads(scores_any_decay)


def _gla(q, k, vt, r, a, wg, bg, gh):
    t = q.shape[0]
    c = GLA_CHUNK
    const = lambda shape: pl.BlockSpec(shape, lambda i: (0, 0))
    last = t // c - 1
    return pl.pallas_call(
        _gla_kernel,
        grid=(t // c,),
        in_specs=[pl.BlockSpec((c, GLA_QK), lambda i: (i, 0)),
                  pl.BlockSpec((c, GLA_QK), lambda i: (i, 0)),
                  pl.BlockSpec((GLA_V, c), lambda i: (0, i)),
                  pl.BlockSpec((c, GLA_V), lambda i: (i, 0)),
                  pl.BlockSpec((c, LANES), lambda i: (i, 0)),
                  pl.BlockSpec((c, LANES), lambda i: (jnp.minimum(i + 1, last), 0)),
                  const((LANES, GLA_QK)),
                  const((1, GLA_QK)),
                  const((1, GLA_DV))],
        out_specs=pl.BlockSpec((c, GLA_V), lambda i: (i, 0)),
        out_shape=jax.ShapeDtypeStruct((t, GLA_V), BF16),
        scratch_shapes=[pltpu.VMEM((GLA_HEADS, GLA_DV, GLA_DK), F32),
                        pltpu.VMEM((2, c, GLA_QK), F32),
                        pltpu.SMEM((2,), jnp.int32)],
        compiler_params=_cparams(("arbitrary",)),
        name="gla_scan",
    )(q, k, vt, r, a, a, wg.astype(BF16), bg.reshape(1, GLA_QK), gh.reshape(1, GLA_DV))


def _sb_sum_matrix():
    tk = SB_TK
    n = SB_FUSED * tk
    j = jnp.arange(n)[:, None]
    s = jnp.arange(n)[None, :]
    later = (j > s)
    cols = [later[:, :tk], jnp.ones((n, tk), bool)] + [later[:, i * tk:(i + 1) * tk] for i in range(1, SB_FUSED)]
    return -jnp.concatenate(cols, axis=1).astype(BF16)


def _sb_kernel(q_ref, kt_ref, v_ref, u_ref, o_ref, acc_ref, carry_ref):
    tq, tk = SB_TQ, SB_TK
    rows = SB_GROUP * tq
    nf = SB_FUSED
    step = pl.program_id(1)
    qpos = lax.broadcasted_iota(jnp.int32, (rows, tk), 0) % tq
    kpos = lax.broadcasted_iota(jnp.int32, (rows, tk), 1)
    causal = kpos < qpos

    blocks = []
    for s in range(SB_QB):
        qs = jnp.concatenate([q_ref[s * tq:(s + 1) * tq, g * SB_DH:(g + 1) * SB_DH]
                              for g in range(SB_GROUP)], axis=0)
        blocks.append((s, step * SB_QB + s, qs))

    def tile(s, qs, kt, masked):
        k0 = pl.multiple_of(kt * tk, tk)
        z = _dot(qs, kt_ref[:, pl.ds(k0, tk)])
        sp = jnp.maximum(z, 0.0) + _log1p_exp_neg_abs(z)
        spm = jnp.where(causal, sp, 0.0) if masked else sp
        spb = spm.astype(BF16)
        tot = _dot(spb, u_ref[(nf - 1) * tk:, tk:2 * tk])
        tail = _dot(spb, u_ref[(nf - 1) * tk:, nf * tk:]) + carry_ref[s]
        w = jnp.exp((z - sp) + tail)
        if masked:
            w = jnp.where(causal, w, 0.0)
        acc_ref[s] += _dot(w.astype(BF16), v_ref[pl.ds(k0, tk), :])
        carry = carry_ref[s] + tot
        carry_ref[s] = carry
        return jnp.max(carry)

    def first_fused(s, qi, qs):
        k0 = pl.multiple_of((qi - (nf - 1)) * tk, tk)
        z = _dot(qs, kt_ref[:, pl.ds(k0, nf * tk)])
        sp = jnp.maximum(z, 0.0) + _log1p_exp_neg_abs(z)
        spm = jnp.concatenate([sp[:, :(nf - 1) * tk],
                               jnp.where(causal, sp[:, (nf - 1) * tk:], 0.0)], axis=1)
        spb = spm.astype(BF16)
        first = _dot(spb, u_ref[:, :2 * tk])
        later = _dot(spb[:, tk:], u_ref[tk:, 2 * tk:])
        tot = first[:, tk:]
        w = jnp.exp((z - sp) + jnp.concatenate([first[:, :tk], later], axis=1))
        w = jnp.concatenate([w[:, :(nf - 1) * tk],
                             jnp.where(causal, w[:, (nf - 1) * tk:], 0.0)], axis=1)
        acc_ref[s] = _dot(w.astype(BF16), v_ref[pl.ds(k0, nf * tk), :])
        carry_ref[s] = tot
        return qi - nf, jnp.max(tot)

    def first_single(s, qi, qs):
        acc_ref[s] = jnp.zeros((rows, SB_DH), F32)
        carry_ref[s] = jnp.zeros((rows, tk), F32)
        return qi - 1, tile(s, qs, qi, True)

    def all_fused():
        return tuple(first_fused(*blk) for blk in blocks)

    def all_single():
        return tuple(first_single(*blk) for blk in blocks)

    starts = lax.cond(step * SB_QB >= nf - 1, all_fused, all_single)

    def unfinished(state):
        kt, mx = state
        return jnp.logical_and(kt >= 0, mx > SB_EXIT)

    for (s, _, qs), start in zip(blocks, starts):
        def earlier_tile(state, s=s, qs=qs):
            kt, _ = state
            return kt - 1, tile(s, qs, kt, False)

        lax.while_loop(unfinished, earlier_tile, start)
        for g in range(SB_GROUP):
            o_ref[s * tq:(s + 1) * tq, g * SB_DH:(g + 1) * SB_DH] = (
                acc_ref[s, g * tq:(g + 1) * tq, :].astype(o_ref.dtype))


def _sb_attention(q, kt, v):
    t = q.shape[0]
    tq = SB_TQ * SB_QB
    gw = SB_GROUP * SB_DH
    u = _sb_sum_matrix()
    return pl.pallas_call(
        _sb_kernel,
        grid=(SB_KVH, t // tq),
        in_specs=[pl.BlockSpec((tq, gw), lambda n, i: (i, n)),
                  pl.BlockSpec((SB_DH, t), lambda n, i: (n, 0)),
                  pl.BlockSpec((t, SB_DH), lambda n, i: (0, n)),
                  pl.BlockSpec(u.shape, lambda n, i: (0, 0))],
        out_specs=pl.BlockSpec((tq, gw), lambda n, i: (i, n)),
        out_shape=jax.ShapeDtypeStruct((t, SB_HEADS * SB_DH), BF16),
        scratch_shapes=[pltpu.VMEM((SB_QB, SB_GROUP * SB_TQ, SB_DH), F32),
                        pltpu.VMEM((SB_QB, SB_GROUP * SB_TQ, SB_TK), F32)],
        compiler_params=_cparams(("arbitrary", "arbitrary")),
        name="stick_breaking",
    )(q, kt, v, u)


def _row_tile(t, want):
    return want if t % want == 0 else t


def kernel(x, norm1_g, norm2_g, gla_w_in, gla_w_gate_up, gla_b_gate, gla_head_g, gla_w_out,
           kv_norm_g, kv_w, sb_w_q, sb_w_out, mlp_w_up, mlp_w_down, final_g):
    b, t, d = x.shape
    assert b == 1 and d == D_MODEL and t % GLA_CHUNK == 0
    x2 = x.reshape(t, d)
    tm_big = _row_tile(t, 1024)
    tn = 2048
    tm = _row_tile(t, 512)
    tf = 1024

    o_k = GLA_QK
    o_v = 2 * GLA_QK
    o_r = o_v + GLA_V
    o_a = o_r + GLA_V
    w_in = gla_w_in[0].astype(BF16)
    w_a = jnp.pad(w_in[:, o_a:], ((0, 0), (0, LANES - GLA_RANK)))
    w_g = jnp.pad(gla_w_gate_up[0], ((0, LANES - GLA_RANK), (0, 0)))

    kv_w3 = kv_w.reshape(1, *kv_w.shape)

    q, xn, w_out0 = _norm_mm(x2, norm1_g[0], w_in, F32, tm=tm_big, cols=(0, GLA_QK),
                             casts=[(gla_w_out, 0)], name="gla_q")
    k, w_q1 = _mm(xn, w_in, F32, tm=tm_big, tn=tn, cols=(o_k, GLA_QK),
                  casts=[(sb_w_q, 0)], name="gla_k")
    vt, w_up0 = _mm(xn, w_in, BF16, tm=tm_big, tn=tn, cols=(o_v, GLA_V), transposed=True,
                    casts=[(mlp_w_up, 0)], name="gla_vt")
    r, w_down0 = _mm(xn, w_in, BF16, tm=tm_big, tn=tn, cols=(o_r, GLA_V),
                     casts=[(mlp_w_down, 0)], name="gla_r")
    a = _mm(xn, w_a, BF16, tm=tm_big, tn=LANES, name="gla_a")
    y = _gla(q, k, vt, r, a, w_g, gla_b_gate[0], gla_head_g[0])
    h, xn, w_out1, w_kv = _mm_res_norm(y, w_out0, x2, norm2_g[0], tm=tm,
                                       casts=[(sb_w_out, 0), (kv_w3, 0)], name="gla_out")
    h, xn_q, xn_kv = _mlp(xn, h, w_up0, w_down0, norm1_g[1], kv_norm_g,
                          tm=tm, tf=tf, final=False, name="mlp0")

    q2, w_up1 = _mm(xn_q, w_q1, BF16, tm=tm_big, tn=tn, scale=SB_DH ** -0.5,
                    casts=[(mlp_w_up, 1)], name="sb_q")
    kt = _mm(xn_kv, w_kv, BF16, tm=tm_big, tn=tn, cols=(0, SB_KV), transposed=True, name="shared_kt")
    v = _mm(xn_kv, w_kv, BF16, tm=tm_big, tn=tn, cols=(SB_KV, SB_KV), name="shared_v")
    o = _sb_attention(q2, kt, v)
    h, xn, w_down1 = _mm_res_norm(o, w_out1, h, norm2_g[1], tm=tm,
                                  casts=[(mlp_w_down, 1)], name="sb_out")
    out, = _mlp(xn, h, w_up1, w_down1, final_g, final_g, tm=tm, tf=tf, final=True, name="mlp1")
    return out.reshape(b, t, d)
```
